```python
import jax, jax.numpy as jnp
from jax import lax
import numpy as np

D_MODEL = 1024
BATCH = 8
SEQ = 4096
DEPTH = 1

CONV_WIDTH_A = D_MODEL // 2
CONV_K = 3
N_HEADS = 8
HEAD_DIM = 64
N_KV_GROUPS = 2
HEADS_PER_GROUP = N_HEADS // N_KV_GROUPS
ATTN_WIDTH = N_HEADS * HEAD_DIM
KV_WIDTH = N_KV_GROUPS * HEAD_DIM
CMP_BLOCK = 32
CMP_STRIDE = 16
SEL_BLOCK = 64
N_SEL = 16
WINDOW = 512
Q_BLOCK = 128
N_NSA_BRANCH = 3
ROPE_THETA = 10000.0
D_FF = 2816
N_MERGE = 2
EPS = 1e-6
NEG = -1e30
FORCE = 1e9

SPLIT_SIZES = [CONV_WIDTH_A, CONV_WIDTH_A, CONV_WIDTH_A, ATTN_WIDTH] + [KV_WIDTH] * 6 + [N_NSA_BRANCH * N_HEADS, N_MERGE * D_MODEL]
IN_COLS = sum(SPLIT_SIZES)

kernel_name = "hybrid_shortconv_nsa_convffn_block"


def rmsnorm(x, g):
    x32 = x.astype(jnp.float32)
    y = x32 * lax.rsqrt(jnp.mean(x32 * x32, axis=-1, keepdims=True) + EPS)
    return (y * g.astype(jnp.float32)).astype(x.dtype)


def causal_dwconv(u, w):
    s = u.shape[1]
    up = jnp.pad(u, ((0, 0), (CONV_K - 1, 0), (0, 0)))
    y = w[0] * up[:, 0:s]
    for k in range(1, CONV_K):
        y = y + w[k] * up[:, k:k + s]
    return y


def rope(x, positions):
    half = HEAD_DIM // 2
    inv = ROPE_THETA ** (-jnp.arange(half, dtype=jnp.float32) / half)
    ang = positions.astype(jnp.float32)[..., None] * inv
    cos = jnp.cos(ang)[:, :, None, :]
    sin = jnp.sin(ang)[:, :, None, :]
    x32 = x.astype(jnp.float32)
    x1, x2 = x32[..., :half], x32[..., half:]
    out = jnp.concatenate([x1 * cos - x2 * sin, x2 * cos + x1 * sin], axis=-1)
    return out.astype(x.dtype)


def compress(kv, pe, w1, w2):
    b, s = kv.shape[0], kv.shape[1]
    n_cmp = (s - CMP_BLOCK) // CMP_STRIDE + 1
    idx = np.arange(n_cmp)[:, None] * CMP_STRIDE + np.arange(CMP_BLOCK)[None, :]
    blk = kv[:, idx] + pe[None, None, :, None, :]
    blk = blk.transpose(0, 1, 3, 2, 4).reshape(b, n_cmp, N_KV_GROUPS, CMP_BLOCK * HEAD_DIM)
    return jax.nn.silu(blk @ w1) @ w2


def masked_softmax(s, mask):
    p = jax.nn.softmax(jnp.where(mask, s, NEG), axis=-1)
    return jnp.where(mask, p, 0.0)


def nsa(q, k_cmp, v_cmp, k_sel, v_sel, k_win, v_win, gates):
    b, s = q.shape[0], q.shape[1]
    n_cmp = k_cmp.shape[1]
    n_blk = s // SEL_BLOCK
    n_sel = min(N_SEL, n_blk)
    n_qb = s // Q_BLOCK
    scale = HEAD_DIM ** -0.5
    cmp_end = jnp.arange(n_cmp) * CMP_STRIDE + CMP_BLOCK - 1
    ii = np.arange(n_cmp)[:, None] * CMP_STRIDE
    jj = np.arange(n_blk)[None, :] * SEL_BLOCK
    overlap = jnp.asarray(((ii < jj + SEL_BLOCK) & (ii + CMP_BLOCK > jj)).astype(np.float32))
    ksb = k_sel.reshape(b, n_blk, SEL_BLOCK, N_KV_GROUPS, HEAD_DIM).transpose(0, 3, 1, 2, 4)
    vsb = v_sel.reshape(b, n_blk, SEL_BLOCK, N_KV_GROUPS, HEAD_DIM).transpose(0, 3, 1, 2, 4)
    kwp = jnp.pad(k_win, ((0, 0), (WINDOW, 0), (0, 0), (0, 0)))
    vwp = jnp.pad(v_win, ((0, 0), (WINDOW, 0), (0, 0), (0, 0)))
    b_ix = jnp.arange(b)[:, None, None, None]
    g_ix = jnp.arange(N_KV_GROUPS)[None, :, None, None]
    blk_ids = jnp.arange(n_blk)

    def query_block(qb):
        t0 = qb * Q_BLOCK
        t = t0 + jnp.arange(Q_BLOCK)
        qc = lax.dynamic_slice_in_dim(q, t0, Q_BLOCK, axis=1).reshape(b, Q_BLOCK, N_KV_GROUPS, HEADS_PER_GROUP, HEAD_DIM)
        gc = lax.dynamic_slice_in_dim(gates, t0, Q_BLOCK, axis=1).reshape(b, Q_BLOCK, N_KV_GROUPS, HEADS_PER_GROUP, N_NSA_BRANCH)
        sc = jnp.einsum('bqghd,bkgd->bghqk', qc, k_cmp).astype(jnp.float32) * scale
        p_c = masked_softmax(sc, cmp_end[None, :] <= t[:, None])
        o_c = jnp.einsum('bghqk,bkgd->bqghd', p_c.astype(v_cmp.dtype), v_cmp)
        imp = jnp.einsum('bghqk,kj->bgqj', p_c, overlap)
        cur = t // SEL_BLOCK
        forced = (blk_ids[None, :] == 0) | (blk_ids[None, :] == cur[:, None]) | (blk_ids[None, :] == cur[:, None] - 1)
        valid = blk_ids[None, :] <= cur[:, None]
        imp = jnp.where(forced, FORCE, jnp.where(valid, imp, -FORCE))
        _, sel = lax.top_k(imp, n_sel)
        ks = ksb[b_ix, g_ix, sel].reshape(b, N_KV_GROUPS, Q_BLOCK, n_sel * SEL_BLOCK, HEAD_DIM)
        vs = vsb[b_ix, g_ix, sel].reshape(b, N_KV_GROUPS, Q_BLOCK, n_sel * SEL_BLOCK, HEAD_DIM)
        kpos = (sel[..., None] * SEL_BLOCK + jnp.arange(SEL_BLOCK)).reshape(b, N_KV_GROUPS, Q_BLOCK, n_sel * SEL_BLOCK)
        ss = jnp.einsum('bqghd,bgqkd->bghqk', qc, ks).astype(jnp.float32) * scale
        p_s = masked_softmax(ss, (kpos <= t[None, None, :, None])[:, :, None])
        o_s = jnp.einsum('bghqk,bgqkd->bqghd', p_s.astype(vs.dtype), vs)
        kw = lax.dynamic_slice_in_dim(kwp, t0, WINDOW + Q_BLOCK, axis=1)
        vw = lax.dynamic_slice_in_dim(vwp, t0, WINDOW + Q_BLOCK, axis=1)
        spos = t0 - WINDOW + jnp.arange(WINDOW + Q_BLOCK)
        diff = t[:, None] - spos[None, :]
        mask_w = (diff >= 0) & (diff < WINDOW) & (spos[None, :] >= 0)
        sw = jnp.einsum('bqghd,bkgd->bghqk', qc, kw).astype(jnp.float32) * scale
        p_w = masked_softmax(sw, mask_w)
        o_w = jnp.einsum('bghqk,bkgd->bqghd', p_w.astype(vw.dtype), vw)
        out = gc[..., 0:1] * o_c + gc[..., 1:2] * o_s + gc[..., 2:3] * o_w
        return out.reshape(b, Q_BLOCK, ATTN_WIDTH)

    out = lax.map(query_block, jnp.arange(n_qb))
    return out.transpose(1, 0, 2, 3).reshape(b, s, ATTN_WIDTH)


def setup_inputs(seed: int = 0) -> dict:
    key = jax.random.key(seed)
    ks = jax.random.split(key, 24)
    L = DEPTH
    nrm = lambda k, shape, scale: jax.random.normal(k, shape, jnp.float32) * scale
    x = jax.random.normal(ks[0], (BATCH, SEQ, D_MODEL), jnp.float32)
    positions = (jnp.arange(SEQ, dtype=jnp.int32)[None, :] + jax.random.randint(ks[1], (BATCH, 1), 0, 1024, dtype=jnp.int32)).astype(jnp.int32)
    return {
        'x': x,
        'positions': positions,
        'norm1_g': 1.0 + nrm(ks[2], (L, D_MODEL), 0.02),
        'w_in': nrm(ks[3], (L, D_MODEL, IN_COLS), D_MODEL ** -0.5),
        'conv_a_w': nrm(ks[4], (L, CONV_K, CONV_WIDTH_A), CONV_K ** -0.5),
        'w_a_out': nrm(ks[5], (L, CONV_WIDTH_A, D_MODEL), CONV_WIDTH_A ** -0.5),
        'q_norm_g': 1.0 + nrm(ks[6], (L, HEAD_DIM), 0.02),
        'k_norm_g': 1.0 + nrm(ks[7], (L, N_NSA_BRANCH, HEAD_DIM), 0.02),
        'cmp_k_pe': nrm(ks[8], (L, CMP_BLOCK, HEAD_DIM), 0.1),
        'cmp_k_w1': nrm(ks[9], (L, CMP_BLOCK * HEAD_DIM, HEAD_DIM), (CMP_BLOCK * HEAD_DIM) ** -0.5),
        'cmp_k_w2': nrm(ks[10], (L, HEAD_DIM, HEAD_DIM), HEAD_DIM ** -0.5),
        'cmp_v_pe': nrm(ks[11], (L, CMP_BLOCK, HEAD_DIM), 0.1),
        'cmp_v_w1': nrm(ks[12], (L, CMP_BLOCK * HEAD_DIM, HEAD_DIM), (CMP_BLOCK * HEAD_DIM) ** -0.5),
        'cmp_v_w2': nrm(ks[13], (L, HEAD_DIM, HEAD_DIM), HEAD_DIM ** -0.5),
        'w_b_out': nrm(ks[14], (L, ATTN_WIDTH, D_MODEL), ATTN_WIDTH ** -0.5),
        'w_o': nrm(ks[15], (L, D_MODEL, D_MODEL), D_MODEL ** -0.5),
        'norm2_g': 1.0 + nrm(ks[16], (L, D_MODEL), 0.02),
        'w_ffn_in': nrm(ks[17], (L, D_MODEL, 2 * D_FF), D_MODEL ** -0.5),
        'ffn_conv_w': nrm(ks[18], (L, CONV_K, D_FF), CONV_K ** -0.5),
        'ffn_conv_b': nrm(ks[19], (L, D_FF), 0.02),
        'w_ffn_out': nrm(ks[20], (L, D_FF, D_MODEL), D_FF ** -0.5),
    }


def reference(x, positions, norm1_g, w_in, conv_a_w, w_a_out, q_norm_g, k_norm_g, cmp_k_pe, cmp_k_w1, cmp_k_w2, cmp_v_pe, cmp_v_w1, cmp_v_w2, w_b_out, w_o, norm2_g, w_ffn_in, ffn_conv_w, ffn_conv_b, w_ffn_out):
    b, s, _ = x.shape
    cuts = list(np.cumsum(SPLIT_SIZES)[:-1].tolist())
    for l in range(DEPTH):
        h = rmsnorm(x, norm1_g[l])
        proj = h @ w_in[l]
        a_b, a_c, a_x, q, kc, vc, ksl, vsl, kwn, vwn, g_nsa, g_merge = jnp.split(proj, cuts, axis=-1)
        y_a = a_b * causal_dwconv(a_c * a_x, conv_a_w[l])
        q = rope(rmsnorm(q.reshape(b, s, N_HEADS, HEAD_DIM), q_norm_g[l]), positions)
        kv_shape = (b, s, N_KV_GROUPS, HEAD_DIM)
        k_cmp = rmsnorm(compress(kc.reshape(kv_shape), cmp_k_pe[l], cmp_k_w1[l], cmp_k_w2[l]), k_norm_g[l, 0])
        v_cmp = compress(vc.reshape(kv_shape), cmp_v_pe[l], cmp_v_w1[l], cmp_v_w2[l])
        k_sel = rope(rmsnorm(ksl.reshape(kv_shape), k_norm_g[l, 1]), positions)
        k_win = rope(rmsnorm(kwn.reshape(kv_shape), k_norm_g[l, 2]), positions)
        gates = jax.nn.sigmoid(g_nsa).reshape(b, s, N_HEADS, N_NSA_BRANCH)
        y_b = nsa(q, k_cmp, v_cmp, k_sel, vsl.reshape(kv_shape), k_win, vwn.reshape(kv_shape), gates)
        gm = jax.nn.sigmoid(g_merge)
        mix = gm[..., :D_MODEL] * (y_a @ w_a_out[l]) + gm[..., D_MODEL:] * (y_b @ w_b_out[l])
        x = x + mix @ w_o[l]
        h = rmsnorm(x, norm2_g[l])
        gu = h @ w_ffn_in[l]
        g = causal_dwconv(gu[..., :D_FF], ffn_conv_w[l]) + ffn_conv_b[l]
        x = x + (jax.nn.silu(g) * gu[..., D_FF:]) @ w_ffn_out[l]
    return x
```

```python
import functools

import jax
import jax.numpy as jnp
import numpy as np
from jax import lax
from jax.experimental import pallas as pl
from jax.experimental.pallas import tpu as pltpu

F32 = jnp.float32
BF16 = jnp.bfloat16

D_MODEL = 1024
CONV_W = 512
CONV_K = 3
N_HEADS = 8
HEAD_DIM = 64
N_GROUPS = 2
HEADS_PER_GROUP = N_HEADS // N_GROUPS
ATTN_W = N_HEADS * HEAD_DIM
KV_W = N_GROUPS * HEAD_DIM
CMP_BLOCK = 32
CMP_STRIDE = 16
SEL_BLOCK = 64
N_SEL = 16
WINDOW = 512
Q_BLOCK = 128
N_BRANCH = 3
ROPE_THETA = 10000.0
D_FF = 2816
EPS = 1e-6
NEG = -1e30
FORCE = 1e9

LANES = 128
SUBLANES = 8
KEY_TILE = 256
ROW_TILE = 512
GATE_PAD = LANES
VMEM_LIMIT = 56 * 1024 * 1024

_A_END = 3 * CONV_W
_Q_OFF = _A_END
_KV_OFF = _Q_OFF + ATTN_W
_GATE_OFF = _KV_OFF + 6 * KV_W
_GM_OFF = _GATE_OFF + GATE_PAD
_IN_PAD = _GM_OFF + 2 * D_MODEL


def _dot(a, b):
    return jnp.dot(a, b, preferred_element_type=F32)


def _dot_nt(a, b):
    return lax.dot_general(a, b, (((1,), (1,)), ((), ())), preferred_element_type=F32)


def _seg64_sum(x):
    lane = lax.broadcasted_iota(jnp.int32, x.shape, 1)
    s = x
    for k in (1, 2, 4, 8, 16, 32):
        up = pltpu.roll(s, k, axis=1)
        dn = pltpu.roll(s, LANES - k, axis=1)
        s = s + jnp.where((lane & k) != 0, up, dn)
    return s


def _head_norm(x, gain):
    ss = _seg64_sum(x * x) * (1.0 / HEAD_DIM)
    return x * lax.rsqrt(ss + EPS) * gain


def _rope(x, cos, sin_signed):
    lane = lax.broadcasted_iota(jnp.int32, x.shape, 1)
    half = HEAD_DIM // 2
    rot = jnp.where((lane & half) != 0, pltpu.roll(x, half, axis=1), pltpu.roll(x, LANES - half, axis=1))
    return x * cos + rot * sin_signed


def _proj_kernel(x_ref, pos_ref, g1_ref, w_ref, cw_ref, qg_ref, kg_ref, inv_ref,
                 ya_ref, q_ref, kc_ref, vc_ref, ksl_ref, vsl_ref, kwn_ref, vwn_ref, gate_ref, gm_ref,
                 ubuf, *, tiles_per_seq, tm):
    i = pl.program_id(0)

    @pl.when(i % tiles_per_seq == 0)
    def _():
        ubuf[0:SUBLANES, :] = jnp.zeros((SUBLANES, CONV_W), F32)

    x = x_ref[...]
    ms = jnp.mean(x * x, axis=-1, keepdims=True)
    h = (x * lax.rsqrt(ms + EPS) * g1_ref[...]).astype(BF16)

    pa = _dot(h, w_ref[:, 0:_A_END])
    u = pa[:, CONV_W:2 * CONV_W] * pa[:, 2 * CONV_W:3 * CONV_W]
    ubuf[SUBLANES:SUBLANES + tm, :] = u
    cw = cw_ref[...]
    y = (cw[0:1, :] * ubuf[SUBLANES - 2:SUBLANES - 2 + tm, :]
         + cw[1:2, :] * ubuf[SUBLANES - 1:SUBLANES - 1 + tm, :]
         + cw[2:3, :] * u)
    ya_ref[...] = (pa[:, 0:CONV_W] * y).astype(BF16)
    ubuf[0:SUBLANES, :] = ubuf[tm:tm + SUBLANES, :]

    ang = pos_ref[...] * inv_ref[...]
    lane = lax.broadcasted_iota(jnp.int32, ang.shape, 1)
    cos = jnp.cos(ang)
    sin_signed = jnp.where((lane & (HEAD_DIM // 2)) != 0, jnp.sin(ang), -jnp.sin(ang))

    pb = _dot(h, w_ref[:, _Q_OFF:_GM_OFF])
    scale = HEAD_DIM ** -0.5
    for c in range(ATTN_W // LANES):
        xq = pb[:, c * LANES:(c + 1) * LANES]
        qn = _head_norm(xq, qg_ref[:, c * LANES:(c + 1) * LANES])
        q_ref[:, c * LANES:(c + 1) * LANES] = (_rope(qn, cos, sin_signed) * scale).astype(BF16)
    kv = pb[:, ATTN_W:ATTN_W + 6 * KV_W]
    kc_ref[...] = kv[:, 0:KV_W]
    vc_ref[...] = kv[:, KV_W:2 * KV_W]
    ksl = _head_norm(kv[:, 2 * KV_W:3 * KV_W], kg_ref[0:1, :])
    ksl_ref[...] = _rope(ksl, cos, sin_signed).astype(BF16)
    vsl_ref[...] = kv[:, 3 * KV_W:4 * KV_W].astype(BF16)
    kwn = _head_norm(kv[:, 4 * KV_W:5 * KV_W], kg_ref[1:2, :])
    kwn_ref[...] = _rope(kwn, cos, sin_signed).astype(BF16)
    vwn_ref[...] = kv[:, 5 * KV_W:6 * KV_W].astype(BF16)
    gate_ref[...] = jax.nn.sigmoid(pb[:, ATTN_W + 6 * KV_W:])

    gm_ref[...] = jax.nn.sigmoid(_dot(h, w_ref[:, _GM_OFF:_IN_PAD])).astype(BF16)


def _proj_call(x2, posf, g1, w_pad, conv_w, qg, kg, inv, seq):
    n = x2.shape[0]
    tm = min(ROW_TILE, seq)
    row = lambda w: pl.BlockSpec((tm, w), lambda i: (i, 0))
    full = lambda a: pl.BlockSpec(a.shape, lambda i: (0,) * a.ndim)
    out_shapes = (
        jax.ShapeDtypeStruct((n, CONV_W), BF16),
        jax.ShapeDtypeStruct((n, ATTN_W), BF16),
        jax.ShapeDtypeStruct((n, KV_W), F32),
        jax.ShapeDtypeStruct((n, KV_W), F32),
        jax.ShapeDtypeStruct((n, KV_W), BF16),
        jax.ShapeDtypeStruct((n, KV_W), BF16),
        jax.ShapeDtypeStruct((n, KV_W), BF16),
        jax.ShapeDtypeStruct((n, KV_W), BF16),
        jax.ShapeDtypeStruct((n, GATE_PAD), F32),
        jax.ShapeDtypeStruct((n, 2 * D_MODEL), BF16),
    )
    return pl.pallas_call(
        functools.partial(_proj_kernel, tiles_per_seq=seq // tm, tm=tm),
        grid=(n // tm,),
        in_specs=[row(D_MODEL), row(1), full(g1), full(w_pad), full(conv_w), full(qg), full(kg), full(inv)],
        out_specs=[row(s.shape[1]) for s in out_shapes],
        out_shape=out_shapes,
        scratch_shapes=[pltpu.VMEM((tm + SUBLANES, CONV_W), F32)],
        compiler_params=pltpu.CompilerParams(dimension_semantics=("arbitrary",), vmem_limit_bytes=VMEM_LIMIT),
        name="proj",
    )(x2, posf, g1, w_pad, conv_w, qg, kg, inv)


def _compress_kernel(kc_ref, vc_ref, pek_ref, pev_ref, w1k_ref, w2k_ref, w1v_ref, w2v_ref, kg_ref,
                     kcmp_ref, vcmp_ref, buf, *, seq):
    nc = seq // CMP_STRIDE
    buf[seq:seq + CMP_BLOCK, :] = jnp.zeros((CMP_BLOCK, KV_W), F32)
    for src, pe, w1, w2, out, is_key in ((kc_ref, pek_ref, w1k_ref, w2k_ref, kcmp_ref, True),
                                         (vc_ref, pev_ref, w1v_ref, w2v_ref, vcmp_ref, False)):
        buf[0:seq, :] = src[0]
        acc = jnp.zeros((nc, KV_W), F32)
        for r in range(CMP_BLOCK):
            xr = buf[pl.ds(r, nc, stride=CMP_STRIDE), :] + pe[r:r + 1, :]
            acc = acc + _dot(xr.astype(BF16), w1[r])
        hmid = acc * jax.nn.sigmoid(acc)
        o = _dot(hmid.astype(BF16), w2[...])
        if is_key:
            o = _head_norm(o, kg_ref[...])
        out[0] = o.astype(BF16)


def _compress_call(kc, vc, pek, pev, w1k, w2k, w1v, w2v, kg0):
    b, seq, _ = kc.shape
    nc = seq // CMP_STRIDE
    per_b = lambda rows: pl.BlockSpec((1, rows, KV_W), lambda i: (i, 0, 0))
    full = lambda a: pl.BlockSpec(a.shape, lambda i: (0,) * a.ndim)
    return pl.pallas_call(
        functools.partial(_compress_kernel, seq=seq),
        grid=(b,),
        in_specs=[per_b(seq), per_b(seq), full(pek), full(pev), full(w1k), full(w2k), full(w1v), full(w2v), full(kg0)],
        out_specs=[per_b(nc), per_b(nc)],
        out_shape=(jax.ShapeDtypeStruct((b, nc, KV_W), BF16), jax.ShapeDtypeStruct((b, nc, KV_W), BF16)),
        scratch_shapes=[pltpu.VMEM((seq + CMP_BLOCK, KV_W), F32)],
        compiler_params=pltpu.CompilerParams(dimension_semantics=("arbitrary",), vmem_limit_bytes=VMEM_LIMIT),
        name="compress",
    )(kc, vc, pek, pev, w1k, w2k, w1v, w2v, kg0)


def _nsa_kernel(q_ref, gate_ref, kcmp_ref, vcmp_ref, ksel_ref, vsel_ref, kwin_ref, vwin_ref, ovt_ref, exp_ref,
                yb_ref, *, seq, n_sel, wlen):
    qb = pl.program_id(1)
    t0 = qb * Q_BLOCK
    nc = seq // CMP_STRIDE
    nb = seq // SEL_BLOCK
    rows = HEADS_PER_GROUP * Q_BLOCK

    q = q_ref[0]
    gates = gate_ref[0]
    lane_q = lax.broadcasted_iota(jnp.int32, (Q_BLOCK, LANES), 1)
    t_col = t0 + lax.broadcasted_iota(jnp.int32, (Q_BLOCK, 1), 0)
    rep = lambda a: jnp.concatenate([a] * HEADS_PER_GROUP, axis=0)

    for g in range(N_GROUPS):
        lo = HEAD_DIM * g
        in_group = (lane_q >= lo) & (lane_q < lo + HEAD_DIM)
        parts = []
        for hl in range(HEADS_PER_GROUP):
            pair, side = divmod(hl, 2)
            c0 = g * HEADS_PER_GROUP * HEAD_DIM + pair * LANES
            qp = q[:, c0:c0 + LANES].astype(F32)
            if side != g:
                qp = pltpu.roll(qp, HEAD_DIM, axis=1)
            parts.append(jnp.where(in_group, qp, 0.0).astype(BF16))
        qs = jnp.concatenate(parts, axis=0)

        s = _dot_nt(qs, kcmp_ref[0])
        cend = lax.broadcasted_iota(jnp.int32, (1, nc), 1) * CMP_STRIDE + (CMP_BLOCK - 1)
        mask_c = rep(cend <= t_col)
        s = jnp.where(mask_c, s, NEG)
        m = jnp.max(s, axis=1, keepdims=True)
        p = jnp.where(mask_c, jnp.exp(s - m), 0.0)
        l = jnp.sum(p, axis=1, keepdims=True)
        pn = p * jnp.where(l > 0.0, 1.0 / l, 0.0)
        o_c = _dot(pn.astype(BF16), vcmp_ref[0])

        ps = pn[0:Q_BLOCK]
        for hl in range(1, HEADS_PER_GROUP):
            ps = ps + pn[hl * Q_BLOCK:(hl + 1) * Q_BLOCK]
        ps_hi = ps.astype(BF16)
        ps_lo = (ps - ps_hi.astype(F32)).astype(BF16)
        imp_t = _dot_nt(ovt_ref[...], ps_hi) + _dot_nt(ovt_ref[...], ps_lo)
        blk = lax.broadcasted_iota(jnp.int32, (nb, Q_BLOCK), 0)
        cur = (t0 + lax.broadcasted_iota(jnp.int32, (nb, Q_BLOCK), 1)) // SEL_BLOCK
        forced = (blk == 0) | (blk == cur) | (blk == cur - 1)
        imp_a = jnp.where(forced, FORCE, jnp.where(blk <= cur, imp_t, -FORCE))
        rank = jnp.zeros((nb, Q_BLOCK), jnp.int32)
        for jp in range(nb):
            other = imp_a[jp:jp + 1, :]
            before = (other > imp_a) | ((other == imp_a) & (blk > jp))
            rank = rank + before.astype(jnp.int32)
        sel = (rank < n_sel).astype(F32).T.astype(BF16)

        def sel_step(kt, carry):
            m_i, l_i, acc = carry
            k0 = pl.multiple_of(kt * KEY_TILE, KEY_TILE)
            s_t = _dot_nt(qs, ksel_ref[0, pl.ds(k0, KEY_TILE), :])
            chosen = _dot(sel, exp_ref[kt])
            kpos = k0 + lax.broadcasted_iota(jnp.int32, (1, KEY_TILE), 1)
            bias = jnp.where((chosen > 0.5) & (kpos <= t_col), 0.0, NEG)
            s_t = s_t + rep(bias)
            m_new = jnp.maximum(m_i, jnp.max(s_t, axis=1, keepdims=True))
            alpha = jnp.exp(m_i - m_new)
            p_t = jnp.exp(s_t - m_new)
            l_new = alpha * l_i + jnp.sum(p_t, axis=1, keepdims=True)
            acc = alpha * acc + _dot(p_t.astype(BF16), vsel_ref[0, pl.ds(k0, KEY_TILE), :])
            return m_new, l_new, acc

        init = (jnp.full((rows, 1), NEG, F32), jnp.zeros((rows, 1), F32), jnp.zeros((rows, LANES), F32))
        _, l_s, acc_s = lax.fori_loop(0, (t0 + Q_BLOCK - 1) // KEY_TILE + 1, sel_step, init)
        o_s = acc_s * (1.0 / l_s)

        start = pl.multiple_of(jnp.maximum(t0 + Q_BLOCK - wlen, 0), Q_BLOCK)
        s = _dot_nt(qs, kwin_ref[0, pl.ds(start, wlen), :])
        diff = t_col - (start + lax.broadcasted_iota(jnp.int32, (1, wlen), 1))
        s = s + rep(jnp.where((diff >= 0) & (diff < WINDOW), 0.0, NEG))
        m = jnp.max(s, axis=1, keepdims=True)
        p = jnp.exp(s - m)
        l = jnp.sum(p, axis=1, keepdims=True)
        o_w = _dot(p.astype(BF16), vwin_ref[0, pl.ds(start, wlen), :]) * (1.0 / l)

        outs = []
        for hl in range(HEADS_PER_GROUP):
            h = g * HEADS_PER_GROUP + hl
            r = slice(hl * Q_BLOCK, (hl + 1) * Q_BLOCK)
            gcol = lambda br: gates[:, N_BRANCH * h + br:N_BRANCH * h + br + 1]
            outs.append(gcol(0) * o_c[r] + gcol(1) * o_s[r] + gcol(2) * o_w[r])
        for pair in range(HEADS_PER_GROUP // 2):
            left, right = outs[2 * pair], outs[2 * pair + 1]
            if g == 0:
                right = pltpu.roll(right, HEAD_DIM, axis=1)
            else:
                left = pltpu.roll(left, HEAD_DIM, axis=1)
            c0 = g * HEADS_PER_GROUP * HEAD_DIM + pair * LANES
            yb_ref[0, :, c0:c0 + LANES] = jnp.where(lane_q < HEAD_DIM, left, right).astype(BF16)


def _nsa_call(q, gates, kcmp, vcmp, ksl, vsl, kwn, vwn, ovt, expand):
    b, seq, _ = q.shape
    nc = seq // CMP_STRIDE
    wlen = min(WINDOW + Q_BLOCK, seq)
    n_sel = min(N_SEL, seq // SEL_BLOCK)
    qblk = lambda w: pl.BlockSpec((1, Q_BLOCK, w), lambda i, j: (i, j, 0))
    per_b = lambda rows: pl.BlockSpec((1, rows, KV_W), lambda i, j: (i, 0, 0))
    full = lambda a: pl.BlockSpec(a.shape, lambda i, j: (0,) * a.ndim)
    return pl.pallas_call(
        functools.partial(_nsa_kernel, seq=seq, n_sel=n_sel, wlen=wlen),
        grid=(b, seq // Q_BLOCK),
        in_specs=[qblk(ATTN_W), qblk(GATE_PAD), per_b(nc), per_b(nc), per_b(seq), per_b(seq), per_b(seq), per_b(seq),
                  full(ovt), full(expand)],
        out_specs=qblk(ATTN_W),
        out_shape=jax.ShapeDtypeStruct((b, seq, ATTN_W), BF16),
        compiler_params=pltpu.CompilerParams(dimension_semantics=("arbitrary", "arbitrary"),
                                             vmem_limit_bytes=VMEM_LIMIT),
        name="nsa",
    )(q, gates, kcmp, vcmp, ksl, vsl, kwn, vwn, ovt, expand)


def _merge_kernel(x_ref, ya_ref, yb_ref, gm_ref, wa_ref, wb_ref, wo_ref, out_ref):
    gm = gm_ref[...].astype(F32)
    mix = gm[:, 0:D_MODEL] * _dot(ya_ref[...], wa_ref[...]) + gm[:, D_MODEL:] * _dot(yb_ref[...], wb_ref[...])
    out_ref[...] = x_ref[...] + _dot(mix.astype(BF16), wo_ref[...])


def _merge_call(x2, ya, yb, gm, wa, wb, wo, seq):
    n = x2.shape[0]
    tm = min(ROW_TILE, seq)
    row = lambda w: pl.BlockSpec((tm, w), lambda i: (i, 0))
    full = lambda a: pl.BlockSpec(a.shape, lambda i: (0,) * a.ndim)
    return pl.pallas_call(
        _merge_kernel,
        grid=(n // tm,),
        in_specs=[row(D_MODEL), row(CONV_W), row(ATTN_W), row(2 * D_MODEL), full(wa), full(wb), full(wo)],
        out_specs=row(D_MODEL),
        out_shape=jax.ShapeDtypeStruct((n, D_MODEL), F32),
        compiler_params=pltpu.CompilerParams(dimension_semantics=("arbitrary",), vmem_limit_bytes=VMEM_LIMIT),
        name="merge",
    )(x2, ya, yb, gm, wa, wb, wo)


_FF_CHUNKS = ((0, 1024), (1024, 1024), (2048, 768))


def _ffn_kernel(x_ref, g2_ref, win_ref, cw_ref, cb_ref, wout_ref, out_ref, gbuf, *, tiles_per_seq, tm):
    i = pl.program_id(0)

    @pl.when(i % tiles_per_seq == 0)
    def _():
        gbuf[0:SUBLANES, :] = jnp.zeros((SUBLANES, D_FF), F32)

    x = x_ref[...]
    ms = jnp.mean(x * x, axis=-1, keepdims=True)
    h = (x * lax.rsqrt(ms + EPS) * g2_ref[...]).astype(BF16)
    acc = jnp.zeros((tm, D_MODEL), F32)
    for c0, w in _FF_CHUNKS:
        cols = slice(c0, c0 + w)
        gpre = _dot(h, win_ref[:, c0:c0 + w])
        up = _dot(h, win_ref[:, D_FF + c0:D_FF + c0 + w])
        gbuf[SUBLANES:SUBLANES + tm, cols] = gpre
        cw = cw_ref[:, cols]
        conv = (cw[0:1, :] * gbuf[SUBLANES - 2:SUBLANES - 2 + tm, cols]
                + cw[1:2, :] * gbuf[SUBLANES - 1:SUBLANES - 1 + tm, cols]
                + cw[2:3, :] * gpre) + cb_ref[:, cols]
        gbuf[0:SUBLANES, cols] = gbuf[tm:tm + SUBLANES, cols]
        act = conv * jax.nn.sigmoid(conv) * up
        acc = acc + _dot(act.astype(BF16), wout_ref[c0:c0 + w, :])
    out_ref[...] = x + acc


def _ffn_call(x1, g2, win, conv_w, conv_b, wout, seq):
    n = x1.shape[0]
    tm = min(ROW_TILE, seq)
    row = lambda w: pl.BlockSpec((tm, w), lambda i: (i, 0))
    full = lambda a: pl.BlockSpec(a.shape, lambda i: (0,) * a.ndim)
    return pl.pallas_call(
        functools.partial(_ffn_kernel, tiles_per_seq=seq // tm, tm=tm),
        grid=(n // tm,),
        in_specs=[row(D_MODEL), full(g2), full(win), full(conv_w), full(conv_b), full(wout)],
        out_specs=row(D_MODEL),
        out_shape=jax.ShapeDtypeStruct((n, D_MODEL), F32),
        scratch_shapes=[pltpu.VMEM((tm + SUBLANES, D_FF), F32)],
        compiler_params=pltpu.CompilerParams(dimension_semantics=("arbitrary",), vmem_limit_bytes=VMEM_LIMIT),
        name="ffn",
    )(x1, g2, win, conv_w, conv_b, wout)


def _block_diag2(w):
    z = jnp.zeros_like(w)
    return jnp.concatenate([jnp.concatenate([w, z], axis=-1), jnp.concatenate([z, w], axis=-1)], axis=-2)


def _static_tables(seq):
    nc, nb = seq // CMP_STRIDE, seq // SEL_BLOCK
    n_cmp = (seq - CMP_BLOCK) // CMP_STRIDE + 1
    ii = np.arange(nc)[None, :] * CMP_STRIDE
    jj = np.arange(nb)[:, None] * SEL_BLOCK
    ovt = ((ii < jj + SEL_BLOCK) & (ii + CMP_BLOCK > jj) & (np.arange(nc)[None, :] < n_cmp)).astype(np.float32)
    kt = np.arange(seq // KEY_TILE)[:, None, None]
    j = np.arange(nb)[None, :, None]
    c = np.arange(KEY_TILE)[None, None, :]
    expand = (j == (kt * KEY_TILE + c) // SEL_BLOCK).astype(np.float32)
    return jnp.asarray(ovt, BF16), jnp.asarray(expand, BF16)


def kernel(x, positions, norm1_g, w_in, conv_a_w, w_a_out, q_norm_g, k_norm_g, cmp_k_pe, cmp_k_w1, cmp_k_w2, cmp_v_pe, cmp_v_w1, cmp_v_w2, w_b_out, w_o, norm2_g, w_ffn_in, ffn_conv_w, ffn_conv_b, w_ffn_out):
    b, seq, d = x.shape
    n = b * seq
    half = HEAD_DIM // 2
    inv = ROPE_THETA ** (-jnp.arange(half, dtype=F32) / half)
    inv_lanes = jnp.tile(inv, LANES // half)[None, :]
    posf = positions.astype(F32).reshape(n, 1)
    ovt, expand = _static_tables(seq)
    gate_cols = N_BRANCH * N_HEADS

    x2 = x.reshape(n, d)
    for l in range(w_in.shape[0]):
        w = w_in[l]
        w_pad = jnp.concatenate(
            [w[:, :_GATE_OFF + gate_cols], jnp.zeros((d, GATE_PAD - gate_cols), w.dtype), w[:, _GATE_OFF + gate_cols:]],
            axis=1).astype(BF16)
        qg = jnp.tile(q_norm_g[l], N_HEADS)[None, :]
        kg = jnp.tile(k_norm_g[l], (1, N_GROUPS))
        ya, q, kc, vc, ksl, vsl, kwn, vwn, gates, gm = _proj_call(
            x2, posf, norm1_g[l][None, :], w_pad, conv_a_w[l], qg, kg[1:3], inv_lanes, seq)

        per_b = lambda a: a.reshape(b, seq, a.shape[-1])
        w1_blocks = lambda w1: _block_diag2(w1.reshape(CMP_BLOCK, HEAD_DIM, HEAD_DIM)).astype(BF16)
        kcmp, vcmp = _compress_call(
            per_b(kc), per_b(vc), jnp.tile(cmp_k_pe[l], (1, N_GROUPS)), jnp.tile(cmp_v_pe[l], (1, N_GROUPS)),
            w1_blocks(cmp_k_w1[l]), _block_diag2(cmp_k_w2[l]).astype(BF16),
            w1_blocks(cmp_v_w1[l]), _block_diag2(cmp_v_w2[l]).astype(BF16), kg[0:1])

        yb = _nsa_call(per_b(q), per_b(gates), kcmp, vcmp, per_b(ksl), per_b(vsl), per_b(kwn), per_b(vwn), ovt, expand)

        x2 = _merge_call(x2, ya, yb.reshape(n, ATTN_W), gm, w_a_out[l].astype(BF16), w_b_out[l].astype(BF16),
                         w_o[l].astype(BF16), seq)
        x2 = _ffn_call(x2, norm2_g[l][None, :], w_ffn_in[l].astype(BF16), ffn_conv_w[l], ffn_conv_b[l][None, :],
                       w_ffn_out[l].astype(BF16), seq)
    return x2.reshape(b, seq, d)
```

```python
import functools

import jax
import jax.numpy as jnp
import numpy as np
from jax import lax
from jax.experimental import pallas as pl
from jax.experimental.pallas import tpu as pltpu

F32 = jnp.float32
BF16 = jnp.bfloat16

D_MODEL = 1024
CONV_W = 512
CONV_K = 3
N_HEADS = 8
HEAD_DIM = 64
N_GROUPS = 2
HEADS_PER_GROUP = N_HEADS // N_GROUPS
ATTN_W = N_HEADS * HEAD_DIM
KV_W = N_GROUPS * HEAD_DIM
CMP_BLOCK = 32
CMP_STRIDE = 16
SEL_BLOCK = 64
N_SEL = 16
WINDOW = 512
Q_BLOCK = 128
N_BRANCH = 3
ROPE_THETA = 10000.0
D_FF = 2816
EPS = 1e-6
NEG = -1e30
FORCE = 1e9
LOG2E = 1.4426950408889634

LANES = 128
SUBLANES = 8
KEY_TILE = 256
ROW_TILE = 512
GATE_PAD = LANES
VMEM_LIMIT = 56 * 1024 * 1024

_A_END = 3 * CONV_W
_Q_OFF = _A_END
_KV_OFF = _Q_OFF + ATTN_W
_GATE_OFF = _KV_OFF + 6 * KV_W
_GM_OFF = _GATE_OFF + GATE_PAD
_IN_PAD = _GM_OFF + 2 * D_MODEL


def _dot(a, b):
    return jnp.dot(a, b, preferred_element_type=F32)


def _dot_nt(a, b):
    return lax.dot_general(a, b, (((1,), (1,)), ((), ())), preferred_element_type=F32)


def _seg64_sum(x):
    lane = lax.broadcasted_iota(jnp.int32, x.shape, 1)
    s = x
    for k in (1, 2, 4, 8, 16, 32):
        up = pltpu.roll(s, k, axis=1)
        dn = pltpu.roll(s, LANES - k, axis=1)
        s = s + jnp.where((lane & k) != 0, up, dn)
    return s


def _head_norm(x, gain):
    ss = _seg64_sum(x * x) * (1.0 / HEAD_DIM)
    return x * lax.rsqrt(ss + EPS) * gain


def _rope(x, cos, sin_signed):
    lane = lax.broadcasted_iota(jnp.int32, x.shape, 1)
    half = HEAD_DIM // 2
    rot = jnp.where((lane & half) != 0, pltpu.roll(x, half, axis=1), pltpu.roll(x, LANES - half, axis=1))
    return x * cos + rot * sin_signed


def _proj_kernel(x_ref, pos_ref, g1_ref, w_ref, cw_ref, qg_ref, kg_ref, inv_ref,
                 ya_ref, q_ref, kc_ref, vc_ref, ksl_ref, vsl0_ref, vsl1_ref, kwn_ref, vwn0_ref, vwn1_ref, gate_ref,
                 gm_ref, ubuf, *, tiles_per_seq, tm):
    i = pl.program_id(0)

    @pl.when(i % tiles_per_seq == 0)
    def _():
        ubuf[0:SUBLANES, :] = jnp.zeros((SUBLANES, CONV_W), F32)

    x = x_ref[...]
    ms = jnp.mean(x * x, axis=-1, keepdims=True)
    h = (x * lax.rsqrt(ms + EPS) * g1_ref[...]).astype(BF16)

    pa = _dot(h, w_ref[:, 0:_A_END])
    u = pa[:, CONV_W:2 * CONV_W] * pa[:, 2 * CONV_W:3 * CONV_W]
    ubuf[SUBLANES:SUBLANES + tm, :] = u
    cw = cw_ref[...]
    y = (cw[0:1, :] * ubuf[SUBLANES - 2:SUBLANES - 2 + tm, :]
         + cw[1:2, :] * ubuf[SUBLANES - 1:SUBLANES - 1 + tm, :]
         + cw[2:3, :] * u)
    ya_ref[...] = (pa[:, 0:CONV_W] * y).astype(BF16)
    ubuf[0:SUBLANES, :] = ubuf[tm:tm + SUBLANES, :]

    ang = pos_ref[...] * inv_ref[...]
    lane = lax.broadcasted_iota(jnp.int32, ang.shape, 1)
    cos = jnp.cos(ang)
    sin_signed = jnp.where((lane & (HEAD_DIM // 2)) != 0, jnp.sin(ang), -jnp.sin(ang))

    pb = _dot(h, w_ref[:, _Q_OFF:_GM_OFF])
    scale = HEAD_DIM ** -0.5 * LOG2E
    for c in range(ATTN_W // LANES):
        xq = pb[:, c * LANES:(c + 1) * LANES]
        qn = _head_norm(xq, qg_ref[:, c * LANES:(c + 1) * LANES])
        q_ref[:, c * LANES:(c + 1) * LANES] = (_rope(qn, cos, sin_signed) * scale).astype(BF16)
    kv = pb[:, ATTN_W:ATTN_W + 6 * KV_W]
    kc_ref[...] = kv[:, 0:KV_W]
    vc_ref[...] = kv[:, KV_W:2 * KV_W]
    ksl = _head_norm(kv[:, 2 * KV_W:3 * KV_W], kg_ref[0:1, :])
    ksl_ref[...] = _rope(ksl, cos, sin_signed).astype(BF16)
    kwn = _head_norm(kv[:, 4 * KV_W:5 * KV_W], kg_ref[1:2, :])
    kwn_ref[...] = _rope(kwn, cos, sin_signed).astype(BF16)
    first = lane < HEAD_DIM
    for v, out0, out1 in ((kv[:, 3 * KV_W:4 * KV_W], vsl0_ref, vsl1_ref), (kv[:, 5 * KV_W:6 * KV_W], vwn0_ref, vwn1_ref)):
        out0[...] = jnp.where(first, v, 1.0).astype(BF16)
        out1[...] = jnp.where(first, 1.0, v).astype(BF16)
    gate_ref[...] = jax.nn.sigmoid(pb[:, ATTN_W + 6 * KV_W:])

    gm_ref[...] = jax.nn.sigmoid(_dot(h, w_ref[:, _GM_OFF:_IN_PAD])).astype(BF16)


def _proj_call(x2, posf, g1, w_pad, conv_w, qg, kg, inv, seq):
    n = x2.shape[0]
    tm = min(ROW_TILE, seq)
    row = lambda w: pl.BlockSpec((tm, w), lambda i: (i, 0))
    full = lambda a: pl.BlockSpec(a.shape, lambda i: (0,) * a.ndim)
    out_shapes = (
        jax.ShapeDtypeStruct((n, CONV_W), BF16),
        jax.ShapeDtypeStruct((n, ATTN_W), BF16),
        jax.ShapeDtypeStruct((n, KV_W), F32),
        jax.ShapeDtypeStruct((n, KV_W), F32),
        jax.ShapeDtypeStruct((n, KV_W), BF16),
        jax.ShapeDtypeStruct((n, KV_W), BF16),
        jax.ShapeDtypeStruct((n, KV_W), BF16),
        jax.ShapeDtypeStruct((n, KV_W), BF16),
        jax.ShapeDtypeStruct((n, KV_W), BF16),
        jax.ShapeDtypeStruct((n, KV_W), BF16),
        jax.ShapeDtypeStruct((n, GATE_PAD), F32),
        jax.ShapeDtypeStruct((n, 2 * D_MODEL), BF16),
    )
    return pl.pallas_call(
        functools.partial(_proj_kernel, tiles_per_seq=seq // tm, tm=tm),
        grid=(n // tm,),
        in_specs=[row(D_MODEL), row(1), full(g1), full(w_pad), full(conv_w), full(qg), full(kg), full(inv)],
        out_specs=[row(s.shape[1]) for s in out_shapes],
        out_shape=out_shapes,
        scratch_shapes=[pltpu.VMEM((tm + SUBLANES, CONV_W), F32)],
        compiler_params=pltpu.CompilerParams(dimension_semantics=("arbitrary",), vmem_limit_bytes=VMEM_LIMIT),
        name="proj",
    )(x2, posf, g1, w_pad, conv_w, qg, kg, inv)


def _compress_kernel(kc_ref, vc_ref, pek_ref, pev_ref, w1k_ref, w2k_ref, w1v_ref, w2v_ref, kg_ref,
                     kcmp_ref, vcmp_ref, buf, *, seq):
    nc = seq // CMP_STRIDE
    buf[seq:seq + CMP_BLOCK, :] = jnp.zeros((CMP_BLOCK, KV_W), F32)
    for src, pe, w1, w2, out, is_key in ((kc_ref, pek_ref, w1k_ref, w2k_ref, kcmp_ref, True),
                                         (vc_ref, pev_ref, w1v_ref, w2v_ref, vcmp_ref, False)):
        buf[0:seq, :] = src[0]
        acc = jnp.zeros((nc, KV_W), F32)
        for r in range(CMP_BLOCK):
            xr = buf[pl.ds(r, nc, stride=CMP_STRIDE), :] + pe[r:r + 1, :]
            acc = acc + _dot(xr.astype(BF16), w1[r])
        hmid = acc * jax.nn.sigmoid(acc)
        o = _dot(hmid.astype(BF16), w2[...])
        if is_key:
            o = _head_norm(o, kg_ref[...])
        out[0] = o.astype(BF16)


def _compress_call(kc, vc, pek, pev, w1k, w2k, w1v, w2v, kg0):
    b, seq, _ = kc.shape
    nc = seq // CMP_STRIDE
    per_b = lambda rows: pl.BlockSpec((1, rows, KV_W), lambda i: (i, 0, 0))
    full = lambda a: pl.BlockSpec(a.shape, lambda i: (0,) * a.ndim)
    return pl.pallas_call(
        functools.partial(_compress_kernel, seq=seq),
        grid=(b,),
        in_specs=[per_b(seq), per_b(seq), full(pek), full(pev), full(w1k), full(w2k), full(w1v), full(w2v), full(kg0)],
        out_specs=[per_b(nc), per_b(nc)],
        out_shape=(jax.ShapeDtypeStruct((b, nc, KV_W), BF16), jax.ShapeDtypeStruct((b, nc, KV_W), BF16)),
        scratch_shapes=[pltpu.VMEM((seq + CMP_BLOCK, KV_W), F32)],
        compiler_params=pltpu.CompilerParams(dimension_semantics=("arbitrary",), vmem_limit_bytes=VMEM_LIMIT),
        name="compress",
    )(kc, vc, pek, pev, w1k, w2k, w1v, w2v, kg0)


def _loop_pairs(lo, hi, step, carry):
    pairs = (hi - lo) // 2

    def two(i, c):
        return step(lo + 2 * i + 1, step(lo + 2 * i, c))

    return lax.fori_loop(lo + 2 * pairs, hi, step, lax.fori_loop(0, pairs, two, carry))


def _nsa_kernel(q_ref, gate_ref, kcmp_ref, vcmp_ref, ksel_ref, vsel0_ref, vsel1_ref, kwin_ref, vwin0_ref, vwin1_ref,
                ovt_ref, hot_ref, yb_ref, sbuf, *, seq, n_sel, wlen):
    qb = pl.program_id(1)
    t0 = qb * Q_BLOCK
    nc = seq // CMP_STRIDE
    nb = seq // SEL_BLOCK
    rows = HEADS_PER_GROUP * Q_BLOCK
    groups = range(N_GROUPS)
    vsel_refs = (vsel0_ref, vsel1_ref)
    vwin_refs = (vwin0_ref, vwin1_ref)

    q = q_ref[0]
    gates = gate_ref[0]
    lane_q = lax.broadcasted_iota(jnp.int32, (Q_BLOCK, LANES), 1)
    t_col = t0 + lax.broadcasted_iota(jnp.int32, (Q_BLOCK, 1), 0)
    rep = lambda a: jnp.concatenate([a] * HEADS_PER_GROUP, axis=0)

    def normalized(acc):
        return acc * (1.0 / pltpu.roll(acc, HEAD_DIM, axis=1))

    start = pl.multiple_of(jnp.maximum(t0 + Q_BLOCK - wlen, 0), Q_BLOCK)
    diff = t_col - (start + lax.broadcasted_iota(jnp.int32, (1, wlen), 1))
    band = rep(jnp.where((diff >= 0) & (diff < WINDOW), 0.0, NEG))
    kw = kwin_ref[0, pl.ds(start, wlen), :]

    qs, o_c, o_w, lhs = [], [], [], []
    for g in groups:
        in_group = (lane_q >= HEAD_DIM * g) & (lane_q < HEAD_DIM * (g + 1))
        parts = []
        for hl in range(HEADS_PER_GROUP):
            pair, side = divmod(hl, 2)
            c0 = g * HEADS_PER_GROUP * HEAD_DIM + pair * LANES
            qp = q[:, c0:c0 + LANES].astype(F32)
            if side != g:
                qp = pltpu.roll(qp, HEAD_DIM, axis=1)
            parts.append(jnp.where(in_group, qp, 0.0).astype(BF16))
        qs.append(jnp.concatenate(parts, axis=0))

        s = _dot_nt(qs[g], kw) + band
        m = jnp.max(s, axis=1, keepdims=True)
        o_w.append(normalized(_dot(jnp.exp2(s - m).astype(BF16), vwin_refs[g][0, pl.ds(start, wlen), :])))

        s = _dot_nt(qs[g], kcmp_ref[0])
        cend = lax.broadcasted_iota(jnp.int32, (1, nc), 1) * CMP_STRIDE + (CMP_BLOCK - 1)
        mask_c = rep(cend <= t_col)
        s = jnp.where(mask_c, s, NEG)
        m = jnp.max(s, axis=1, keepdims=True)
        p = jnp.where(mask_c, jnp.exp2(s - m), 0.0)
        l = jnp.sum(p, axis=1, keepdims=True)
        pn = p * jnp.where(l > 0.0, 1.0 / l, 0.0)
        o_c.append(_dot(pn.astype(BF16), vcmp_ref[0]))

        ps = pn[0:Q_BLOCK]
        for hl in range(1, HEADS_PER_GROUP):
            ps = ps + pn[hl * Q_BLOCK:(hl + 1) * Q_BLOCK]
        ps_hi = ps.astype(BF16)
        ps_lo = (ps - ps_hi.astype(F32)).astype(BF16)
        imp_t = _dot_nt(ovt_ref[...], ps_hi) + _dot_nt(ovt_ref[...], ps_lo)
        blk = lax.broadcasted_iota(jnp.int32, (nb, Q_BLOCK), 0)
        cur = (t0 + lax.broadcasted_iota(jnp.int32, (nb, Q_BLOCK), 1)) // SEL_BLOCK
        forced = (blk == 0) | (blk == cur) | (blk == cur - 1)
        imp_a = jnp.where(forced, FORCE, jnp.where(blk <= cur, imp_t, -FORCE))
        tiles = [imp_a[r * SUBLANES:(r + 1) * SUBLANES] for r in range(nb // SUBLANES)]
        sub = lax.broadcasted_iota(jnp.int32, (SUBLANES, Q_BLOCK), 0)
        rank = [jnp.zeros((SUBLANES, Q_BLOCK), jnp.int32) for _ in tiles]
        for jp in range(nb):
            other = jnp.broadcast_to(imp_a[jp:jp + 1, :], (SUBLANES, Q_BLOCK))
            for r, tile in enumerate(tiles):
                if r > jp // SUBLANES:
                    before = other >= tile
                elif r < jp // SUBLANES:
                    before = other > tile
                else:
                    before = (other > tile) | ((other == tile) & (sub > jp % SUBLANES))
                rank[r] = rank[r] + before.astype(jnp.int32)
        neg_t = jnp.concatenate([jnp.where(r < n_sel, 0.0, NEG) for r in rank]
                                + [jnp.zeros((LANES - nb, Q_BLOCK), F32)], axis=0)
        lhs.append(jnp.concatenate([qs[g], rep(neg_t.T.astype(BF16))], axis=1))

    def scores(kt):
        k0 = pl.multiple_of(kt * KEY_TILE, KEY_TILE)
        rhs = jnp.concatenate([ksel_ref[0, pl.ds(k0, KEY_TILE), :], hot_ref[pl.ds(k0, KEY_TILE), :]], axis=1)
        return [_dot_nt(lhs[g], rhs) for g in groups]

    def fold_max(m_run, s_t):
        return jnp.maximum(m_run, jnp.maximum(s_t[:, 0:LANES], s_t[:, LANES:KEY_TILE]))

    def pass1(kt, m_run):
        s_t = scores(kt)
        for g in groups:
            sbuf[g, kt] = s_t[g]
        return tuple(fold_max(m_run[g], s_t[g]) for g in groups)

    last = (t0 + Q_BLOCK - 1) // KEY_TILE
    m_run = _loop_pairs(0, last, pass1, tuple(jnp.full((rows, LANES), NEG, F32) for _ in groups))
    k_last = last * KEY_TILE + lax.broadcasted_iota(jnp.int32, (1, KEY_TILE), 1)
    causal = rep(jnp.where(k_last <= t_col, 0.0, NEG))
    s_t = scores(last)
    m_sel = []
    for g in groups:
        s_g = s_t[g] + causal
        sbuf[g, last] = s_g
        m_sel.append(jnp.broadcast_to(jnp.max(fold_max(m_run[g], s_g), axis=1, keepdims=True), (rows, LANES)))

    def pass2(kt, acc):
        k0 = pl.multiple_of(kt * KEY_TILE, KEY_TILE)
        out = []
        for g in groups:
            s_g = sbuf[g, kt]
            p = jnp.concatenate([jnp.exp2(s_g[:, 0:LANES] - m_sel[g]), jnp.exp2(s_g[:, LANES:KEY_TILE] - m_sel[g])],
                                axis=1).astype(BF16)
            out.append(acc[g] + _dot(p, vsel_refs[g][0, pl.ds(k0, KEY_TILE), :]))
        return tuple(out)

    acc_s = _loop_pairs(0, last + 1, pass2, tuple(jnp.zeros((rows, LANES), F32) for _ in groups))

    for g in groups:
        o_s = normalized(acc_s[g])
        outs = []
        for hl in range(HEADS_PER_GROUP):
            h = g * HEADS_PER_GROUP + hl
            r = slice(hl * Q_BLOCK, (hl + 1) * Q_BLOCK)
            gcol = lambda br: gates[:, N_BRANCH * h + br:N_BRANCH * h + br + 1]
            outs.append(gcol(0) * o_c[g][r] + gcol(1) * o_s[r] + gcol(2) * o_w[g][r])
        for pair in range(HEADS_PER_GROUP // 2):
            left, right = outs[2 * pair], outs[2 * pair + 1]
            if g == 0:
                right = pltpu.roll(right, HEAD_DIM, axis=1)
            else:
                left = pltpu.roll(left, HEAD_DIM, axis=1)
            c0 = g * HEADS_PER_GROUP * HEAD_DIM + pair * LANES
            yb_ref[0, :, c0:c0 + LANES] = jnp.where(lane_q < HEAD_DIM, left, right).astype(BF16)


def _nsa_call(q, gates, kcmp, vcmp, ksl, vsl0, vsl1, kwn, vwn0, vwn1, ovt, hot):
    b, seq, _ = q.shape
    nc = seq // CMP_STRIDE
    wlen = min(WINDOW + Q_BLOCK, seq)
    n_sel = min(N_SEL, seq // SEL_BLOCK)
    qblk = lambda w: pl.BlockSpec((1, Q_BLOCK, w), lambda i, j: (i, j, 0))
    per_b = lambda rows: pl.BlockSpec((1, rows, KV_W), lambda i, j: (i, 0, 0))
    full = lambda a: pl.BlockSpec(a.shape, lambda i, j: (0,) * a.ndim)
    return pl.pallas_call(
        functools.partial(_nsa_kernel, seq=seq, n_sel=n_sel, wlen=wlen),
        grid=(b, seq // Q_BLOCK),
        in_specs=[qblk(ATTN_W), qblk(GATE_PAD), per_b(nc), per_b(nc)] + [per_b(seq)] * 6 + [full(ovt), full(hot)],
        out_specs=qblk(ATTN_W),
        out_shape=jax.ShapeDtypeStruct((b, seq, ATTN_W), BF16),
        scratch_shapes=[pltpu.VMEM((N_GROUPS, seq // KEY_TILE, HEADS_PER_GROUP * Q_BLOCK, KEY_TILE), F32)],
        compiler_params=pltpu.CompilerParams(dimension_semantics=("arbitrary", "arbitrary"),
                                             vmem_limit_bytes=VMEM_LIMIT),
        name="nsa",
    )(q, gates, kcmp, vcmp, ksl, vsl0, vsl1, kwn, vwn0, vwn1, ovt, hot)


def _merge_kernel(x_ref, ya_ref, yb_ref, gm_ref, wa_ref, wb_ref, wo_ref, out_ref):
    gm = gm_ref[...].astype(F32)
    mix = gm[:, 0:D_MODEL] * _dot(ya_ref[...], wa_ref[...]) + gm[:, D_MODEL:] * _dot(yb_ref[...], wb_ref[...])
    out_ref[...] = x_ref[...] + _dot(mix.astype(BF16), wo_ref[...])


def _merge_call(x2, ya, yb, gm, wa, wb, wo, seq):
    n = x2.shape[0]
    tm = min(ROW_TILE, seq)
    row = lambda w: pl.BlockSpec((tm, w), lambda i: (i, 0))
    full = lambda a: pl.BlockSpec(a.shape, lambda i: (0,) * a.ndim)
    return pl.pallas_call(
        _merge_kernel,
        grid=(n // tm,),
        in_specs=[row(D_MODEL), row(CONV_W), row(ATTN_W), row(2 * D_MODEL), full(wa), full(wb), full(wo)],
        out_specs=row(D_MODEL),
        out_shape=jax.ShapeDtypeStruct((n, D_MODEL), F32),
        compiler_params=pltpu.CompilerParams(dimension_semantics=("arbitrary",), vmem_limit_bytes=VMEM_LIMIT),
        name="merge",
    )(x2, ya, yb, gm, wa, wb, wo)


_FF_CHUNKS = ((0, 1024), (1024, 1024), (2048, 768))


def _ffn_kernel(x_ref, g2_ref, win_ref, cw_ref, cb_ref, wout_ref, out_ref, gbuf, *, tiles_per_seq, tm):
    i = pl.program_id(0)

    @pl.when(i % tiles_per_seq == 0)
    def _():
        gbuf[0:SUBLANES, :] = jnp.zeros((SUBLANES, D_FF), F32)

    x = x_ref[...]
    ms = jnp.mean(x * x, axis=-1, keepdims=True)
    h = (x * lax.rsqrt(ms + EPS) * g2_ref[...]).astype(BF16)
    acc = jnp.zeros((tm, D_MODEL), F32)
    for c0, w in _FF_CHUNKS:
        cols = slice(c0, c0 + w)
        gpre = _dot(h, win_ref[:, c0:c0 + w])
        up = _dot(h, win_ref[:, D_FF + c0:D_FF + c0 + w])
        gbuf[SUBLANES:SUBLANES + tm, cols] = gpre
        cw = cw_ref[:, cols]
        conv = (cw[0:1, :] * gbuf[SUBLANES - 2:SUBLANES - 2 + tm, cols]
                + cw[1:2, :] * gbuf[SUBLANES - 1:SUBLANES - 1 + tm, cols]
                + cw[2:3, :] * gpre) + cb_ref[:, cols]
        gbuf[0:SUBLANES, cols] = gbuf[tm:tm + SUBLANES, cols]
        act = conv * jax.nn.sigmoid(conv) * up
        acc = acc + _dot(act.astype(BF16), wout_ref[c0:c0 + w, :])
    out_ref[...] = x + acc


def _ffn_call(x1, g2, win, conv_w, conv_b, wout, seq):
    n = x1.shape[0]
    tm = min(ROW_TILE, seq)
    row = lambda w: pl.BlockSpec((tm, w), lambda i: (i, 0))
    full = lambda a: pl.BlockSpec(a.shape, lambda i: (0,) * a.ndim)
    return pl.pallas_call(
        functools.partial(_ffn_kernel, tiles_per_seq=seq // tm, tm=tm),
        grid=(n // tm,),
        in_specs=[row(D_MODEL), full(g2), full(win), full(conv_w), full(conv_b), full(wout)],
        out_specs=row(D_MODEL),
        out_shape=jax.ShapeDtypeStruct((n, D_MODEL), F32),
        scratch_shapes=[pltpu.VMEM((tm + SUBLANES, D_FF), F32)],
        compiler_params=pltpu.CompilerParams(dimension_semantics=("arbitrary",), vmem_limit_bytes=VMEM_LIMIT),
        name="ffn",
    )(x1, g2, win, conv_w, conv_b, wout)


def _block_diag2(w):
    z = jnp.zeros_like(w)
    return jnp.concatenate([jnp.concatenate([w, z], axis=-1), jnp.concatenate([z, w], axis=-1)], axis=-2)


def _static_tables(seq):
    nc, nb = seq // CMP_STRIDE, seq // SEL_BLOCK
    n_cmp = (seq - CMP_BLOCK) // CMP_STRIDE + 1
    ii = np.arange(nc)[None, :] * CMP_STRIDE
    jj = np.arange(nb)[:, None] * SEL_BLOCK
    ovt = ((ii < jj + SEL_BLOCK) & (ii + CMP_BLOCK > jj) & (np.arange(nc)[None, :] < n_cmp)).astype(np.float32)
    hot = (np.arange(LANES)[None, :] == (np.arange(seq) // SEL_BLOCK)[:, None]).astype(np.float32)
    return jnp.asarray(ovt, BF16), jnp.asarray(hot, BF16)


def kernel(x, positions, norm1_g, w_in, conv_a_w, w_a_out, q_norm_g, k_norm_g, cmp_k_pe, cmp_k_w1, cmp_k_w2, cmp_v_pe, cmp_v_w1, cmp_v_w2, w_b_out, w_o, norm2_g, w_ffn_in, ffn_conv_w, ffn_conv_b, w_ffn_out):
    b, seq, d = x.shape
    n = b * seq
    half = HEAD_DIM // 2
    inv = ROPE_THETA ** (-jnp.arange(half, dtype=F32) / half)
    inv_lanes = jnp.tile(inv, LANES // half)[None, :]
    posf = positions.astype(F32).reshape(n, 1)
    ovt, hot = _static_tables(seq)
    gate_cols = N_BRANCH * N_HEADS

    x2 = x.reshape(n, d)
    for l in range(w_in.shape[0]):
        w = w_in[l]
        w_pad = jnp.concatenate(
            [w[:, :_GATE_OFF + gate_cols], jnp.zeros((d, GATE_PAD - gate_cols), w.dtype), w[:, _GATE_OFF + gate_cols:]],
            axis=1).astype(BF16)
        qg = jnp.tile(q_norm_g[l], N_HEADS)[None, :]
        kg = jnp.tile(k_norm_g[l], (1, N_GROUPS))
        ya, q, kc, vc, ksl, vsl0, vsl1, kwn, vwn0, vwn1, gates, gm = _proj_call(
            x2, posf, norm1_g[l][None, :], w_pad, conv_a_w[l], qg, kg[1:3], inv_lanes, seq)

        per_b = lambda a: a.reshape(b, seq, a.shape[-1])
        w1_blocks = lambda w1: _block_diag2(w1.reshape(CMP_BLOCK, HEAD_DIM, HEAD_DIM)).astype(BF16)
        kcmp, vcmp = _compress_call(
            per_b(kc), per_b(vc), jnp.tile(cmp_k_pe[l], (1, N_GROUPS)), jnp.tile(cmp_v_pe[l], (1, N_GROUPS)),
            w1_blocks(cmp_k_w1[l]), _block_diag2(cmp_k_w2[l]).astype(BF16),
            w1_blocks(cmp_v_w1[l]), _block_diag2(cmp_v_w2[l]).astype(BF16), kg[0:1])

        yb = _nsa_call(per_b(q), per_b(gates), kcmp, vcmp, per_b(ksl), per_b(vsl0), per_b(vsl1), per_b(kwn),
                       per_b(vwn0), per_b(vwn1), ovt, hot)

        x2 = _merge_call(x2, ya, yb.reshape(n, ATTN_W), gm, w_a_out[l].astype(BF16), w_b_out[l].astype(BF16),
                         w_o[l].astype(BF16), seq)
        x2 = _ffn_call(x2, norm2_g[l][None, :], w_ffn_in[l].astype(BF16), ffn_conv_w[l], ffn_conv_b[l][None, :],
                       w_ffn_out[l].astype(BF16), seq)
    return x2.reshape(b, seq, d)
```

```python
import functools

import jax
import jax.numpy as jnp
import numpy as np
from jax import lax
from jax.experimental import pallas as pl
from jax.experimental.pallas import tpu as pltpu

F32 = jnp.float32
BF16 = jnp.bfloat16

D_MODEL = 1024
CONV_W = 512
CONV_K = 3
N_HEADS = 8
HEAD_DIM = 64
N_GROUPS = 2
HEADS_PER_GROUP = N_HEADS // N_GROUPS
ATTN_W = N_HEADS * HEAD_DIM
KV_W = N_GROUPS * HEAD_DIM
CMP_BLOCK = 32
CMP_STRIDE = 16
SEL_BLOCK = 64
N_SEL = 16
WINDOW = 512
Q_BLOCK = 128
N_BRANCH = 3
ROPE_THETA = 10000.0
D_FF = 2816
EPS = 1e-6
NEG = -1e30
FORCE = 1e9
LOG2E = 1.4426950408889634
SCORE_BOUND = 60.0

LANES = 128
SUBLANES = 8
KEY_TILE = 256
ROW_TILE = 512
GATE_PAD = LANES
VMEM_LIMIT = 56 * 1024 * 1024

_A_END = 3 * CONV_W
_Q_OFF = _A_END
_KV_OFF = _Q_OFF + ATTN_W
_GATE_OFF = _KV_OFF + 6 * KV_W
_GM_OFF = _GATE_OFF + GATE_PAD
_IN_PAD = _GM_OFF + 2 * D_MODEL


def _dot(a, b):
    return jnp.dot(a, b, preferred_element_type=F32)


def _dot_nt(a, b):
    return lax.dot_general(a, b, (((1,), (1,)), ((), ())), preferred_element_type=F32)


def _seg64_sum(x):
    lane = lax.broadcasted_iota(jnp.int32, x.shape, 1)
    s = x
    for k in (1, 2, 4, 8, 16, 32):
        up = pltpu.roll(s, k, axis=1)
        dn = pltpu.roll(s, LANES - k, axis=1)
        s = s + jnp.where((lane & k) != 0, up, dn)
    return s


def _head_norm(x, gain):
    ss = _seg64_sum(x * x) * (1.0 / HEAD_DIM)
    return x * lax.rsqrt(ss + EPS) * gain


def _rope(x, cos, sin_signed):
    lane = lax.broadcasted_iota(jnp.int32, x.shape, 1)
    half = HEAD_DIM // 2
    rot = jnp.where((lane & half) != 0, pltpu.roll(x, half, axis=1), pltpu.roll(x, LANES - half, axis=1))
    return x * cos + rot * sin_signed


def _proj_kernel(x_ref, pos_ref, g1_ref, w_ref, cw_ref, qg_ref, kg_ref, inv_ref,
                 ya_ref, q_ref, kc_ref, vc_ref, ksl_ref, vsl0_ref, vsl1_ref, kwn_ref, vwn0_ref, vwn1_ref, gate_ref,
                 gm_ref, ubuf, *, tiles_per_seq, tm):
    i = pl.program_id(0)

    @pl.when(i % tiles_per_seq == 0)
    def _():
        ubuf[0:SUBLANES, :] = jnp.zeros((SUBLANES, CONV_W), F32)

    x = x_ref[...]
    ms = jnp.mean(x * x, axis=-1, keepdims=True)
    h = (x * lax.rsqrt(ms + EPS) * g1_ref[...]).astype(BF16)

    pa = _dot(h, w_ref[:, 0:_A_END])
    u = pa[:, CONV_W:2 * CONV_W] * pa[:, 2 * CONV_W:3 * CONV_W]
    ubuf[SUBLANES:SUBLANES + tm, :] = u
    cw = cw_ref[...]
    y = (cw[0:1, :] * ubuf[SUBLANES - 2:SUBLANES - 2 + tm, :]
         + cw[1:2, :] * ubuf[SUBLANES - 1:SUBLANES - 1 + tm, :]
         + cw[2:3, :] * u)
    ya_ref[...] = (pa[:, 0:CONV_W] * y).astype(BF16)
    ubuf[0:SUBLANES, :] = ubuf[tm:tm + SUBLANES, :]

    ang = pos_ref[...] * inv_ref[...]
    lane = lax.broadcasted_iota(jnp.int32, ang.shape, 1)
    cos = jnp.cos(ang)
    sin_signed = jnp.where((lane & (HEAD_DIM // 2)) != 0, jnp.sin(ang), -jnp.sin(ang))

    pb = _dot(h, w_ref[:, _Q_OFF:_GM_OFF])
    scale = HEAD_DIM ** -0.5 * LOG2E
    for c in range(ATTN_W // LANES):
        xq = pb[:, c * LANES:(c + 1) * LANES]
        qn = _head_norm(xq, qg_ref[:, c * LANES:(c + 1) * LANES])
        q_ref[:, c * LANES:(c + 1) * LANES] = (_rope(qn, cos, sin_signed) * scale).astype(BF16)
    kv = pb[:, ATTN_W:ATTN_W + 6 * KV_W]
    kc_ref[...] = kv[:, 0:KV_W]
    vc_ref[...] = kv[:, KV_W:2 * KV_W]
    ksl = _head_norm(kv[:, 2 * KV_W:3 * KV_W], kg_ref[0:1, :])
    ksl_ref[...] = _rope(ksl, cos, sin_signed).astype(BF16)
    kwn = _head_norm(kv[:, 4 * KV_W:5 * KV_W], kg_ref[1:2, :])
    kwn_ref[...] = _rope(kwn, cos, sin_signed).astype(BF16)
    first = lane < HEAD_DIM
    for v, out0, out1 in ((kv[:, 3 * KV_W:4 * KV_W], vsl0_ref, vsl1_ref), (kv[:, 5 * KV_W:6 * KV_W], vwn0_ref, vwn1_ref)):
        out0[...] = jnp.where(first, v, 1.0).astype(BF16)
        out1[...] = jnp.where(first, 1.0, v).astype(BF16)
    gate_ref[...] = jax.nn.sigmoid(pb[:, ATTN_W + 6 * KV_W:])

    gm_ref[...] = jax.nn.sigmoid(_dot(h, w_ref[:, _GM_OFF:_IN_PAD])).astype(BF16)


def _proj_call(x2, posf, g1, w_pad, conv_w, qg, kg, inv, seq):
    n = x2.shape[0]
    tm = min(ROW_TILE, seq)
    row = lambda w: pl.BlockSpec((tm, w), lambda i: (i, 0))
    full = lambda a: pl.BlockSpec(a.shape, lambda i: (0,) * a.ndim)
    out_shapes = (
        jax.ShapeDtypeStruct((n, CONV_W), BF16),
        jax.ShapeDtypeStruct((n, ATTN_W), BF16),
        jax.ShapeDtypeStruct((n, KV_W), F32),
        jax.ShapeDtypeStruct((n, KV_W), F32),
        jax.ShapeDtypeStruct((n, KV_W), BF16),
        jax.ShapeDtypeStruct((n, KV_W), BF16),
        jax.ShapeDtypeStruct((n, KV_W), BF16),
        jax.ShapeDtypeStruct((n, KV_W), BF16),
        jax.ShapeDtypeStruct((n, KV_W), BF16),
        jax.ShapeDtypeStruct((n, KV_W), BF16),
        jax.ShapeDtypeStruct((n, GATE_PAD), F32),
        jax.ShapeDtypeStruct((n, 2 * D_MODEL), BF16),
    )
    return pl.pallas_call(
        functools.partial(_proj_kernel, tiles_per_seq=seq // tm, tm=tm),
        grid=(n // tm,),
        in_specs=[row(D_MODEL), row(1), full(g1), full(w_pad), full(conv_w), full(qg), full(kg), full(inv)],
        out_specs=[row(s.shape[1]) for s in out_shapes],
        out_shape=out_shapes,
        scratch_shapes=[pltpu.VMEM((tm + SUBLANES, CONV_W), F32)],
        compiler_params=pltpu.CompilerParams(dimension_semantics=("arbitrary",), vmem_limit_bytes=VMEM_LIMIT),
        name="proj",
    )(x2, posf, g1, w_pad, conv_w, qg, kg, inv)


def _compress_kernel(kc_ref, vc_ref, pek_ref, pev_ref, w1k_ref, w2k_ref, w1v_ref, w2v_ref, kg_ref,
                     kcmp_ref, vcmp0_ref, vcmp1_ref, buf, *, seq):
    nc = seq // CMP_STRIDE
    buf[seq:seq + CMP_BLOCK, :] = jnp.zeros((CMP_BLOCK, KV_W), F32)
    for src, pe, w1, w2, is_key in ((kc_ref, pek_ref, w1k_ref, w2k_ref, True), (vc_ref, pev_ref, w1v_ref, w2v_ref, False)):
        buf[0:seq, :] = src[0]
        acc = jnp.zeros((nc, KV_W), F32)
        for r in range(CMP_BLOCK):
            xr = buf[pl.ds(r, nc, stride=CMP_STRIDE), :] + pe[r:r + 1, :]
            acc = acc + _dot(xr.astype(BF16), w1[r])
        hmid = acc * jax.nn.sigmoid(acc)
        o = _dot(hmid.astype(BF16), w2[...])
        if is_key:
            kcmp_ref[0] = _head_norm(o, kg_ref[...]).astype(BF16)
        else:
            first = lax.broadcasted_iota(jnp.int32, o.shape, 1) < HEAD_DIM
            vcmp0_ref[0] = jnp.where(first, o, 1.0).astype(BF16)
            vcmp1_ref[0] = jnp.where(first, 1.0, o).astype(BF16)


def _compress_call(kc, vc, pek, pev, w1k, w2k, w1v, w2v, kg0):
    b, seq, _ = kc.shape
    nc = seq // CMP_STRIDE
    per_b = lambda rows: pl.BlockSpec((1, rows, KV_W), lambda i: (i, 0, 0))
    full = lambda a: pl.BlockSpec(a.shape, lambda i: (0,) * a.ndim)
    return pl.pallas_call(
        functools.partial(_compress_kernel, seq=seq),
        grid=(b,),
        in_specs=[per_b(seq), per_b(seq), full(pek), full(pev), full(w1k), full(w2k), full(w1v), full(w2v), full(kg0)],
        out_specs=[per_b(nc)] * 3,
        out_shape=(jax.ShapeDtypeStruct((b, nc, KV_W), BF16),) * 3,
        scratch_shapes=[pltpu.VMEM((seq + CMP_BLOCK, KV_W), F32)],
        compiler_params=pltpu.CompilerParams(dimension_semantics=("arbitrary",), vmem_limit_bytes=VMEM_LIMIT),
        name="compress",
    )(kc, vc, pek, pev, w1k, w2k, w1v, w2v, kg0)


def _loop_pairs(lo, hi, step, carry):
    pairs = (hi - lo) // 2

    def two(i, c):
        return step(lo + 2 * i + 1, step(lo + 2 * i, c))

    return lax.fori_loop(lo + 2 * pairs, hi, step, lax.fori_loop(0, pairs, two, carry))


def _nsa_kernel(q_ref, gate_ref, kcmp_ref, vcmp0_ref, vcmp1_ref, ksel_ref, vsel0_ref, vsel1_ref, kwin_ref, vwin0_ref,
                vwin1_ref, ova_ref, hot_ref, yb_ref, *scratch, seq, n_sel, wlen, bounded):
    qb = pl.program_id(1)
    t0 = qb * Q_BLOCK
    nc = seq // CMP_STRIDE
    nb = seq // SEL_BLOCK
    rows = HEADS_PER_GROUP * Q_BLOCK
    groups = range(N_GROUPS)
    vcmp_refs = (vcmp0_ref, vcmp1_ref)
    vsel_refs = (vsel0_ref, vsel1_ref)
    vwin_refs = (vwin0_ref, vwin1_ref)

    q = q_ref[0]
    gates = gate_ref[0]
    lane_q = lax.broadcasted_iota(jnp.int32, (Q_BLOCK, LANES), 1)
    t_col = t0 + lax.broadcasted_iota(jnp.int32, (Q_BLOCK, 1), 0)
    rep = lambda a: jnp.concatenate([a] * HEADS_PER_GROUP, axis=0)

    def weights(s):
        if bounded:
            return jnp.exp2(s)
        m = jnp.maximum(jnp.max(s, axis=1, keepdims=True), 0.5 * NEG)
        return jnp.exp2(s - m)

    def normalized(acc):
        l = pltpu.roll(acc, HEAD_DIM, axis=1)
        return acc * jnp.where(l > 0.0, 1.0 / l, 0.0)

    start = pl.multiple_of(jnp.maximum(t0 + Q_BLOCK - wlen, 0), Q_BLOCK)
    diff = t_col - (start + lax.broadcasted_iota(jnp.int32, (1, wlen), 1))
    band = rep(jnp.where((diff >= 0) & (diff < WINDOW), 0.0, NEG))
    kw = kwin_ref[0, pl.ds(start, wlen), :]
    cend = lax.broadcasted_iota(jnp.int32, (1, nc), 1) * CMP_STRIDE + (CMP_BLOCK - 1)
    done = rep(jnp.where(cend <= t_col, 0.0, NEG))

    qs, o_c, o_w, lhs = [], [], [], []
    for g in groups:
        in_group = (lane_q >= HEAD_DIM * g) & (lane_q < HEAD_DIM * (g + 1))
        parts = []
        for hl in range(HEADS_PER_GROUP):
            pair, side = divmod(hl, 2)
            c0 = g * HEADS_PER_GROUP * HEAD_DIM + pair * LANES
            qp = q[:, c0:c0 + LANES].astype(F32)
            if side != g:
                qp = pltpu.roll(qp, HEAD_DIM, axis=1)
            parts.append(jnp.where(in_group, qp, 0.0).astype(BF16))
        qs.append(jnp.concatenate(parts, axis=0))

        p = weights(_dot_nt(qs[g], kw) + band)
        o_w.append(normalized(_dot(p.astype(BF16), vwin_refs[g][0, pl.ds(start, wlen), :])))

        p = weights(_dot_nt(qs[g], kcmp_ref[0]) + done)
        p_hi = p.astype(BF16)
        p_lo = (p - p_hi.astype(F32)).astype(BF16)
        o_c.append(normalized(_dot(p_hi, vcmp_refs[g][0])))

        share = normalized(_dot(p_hi, ova_ref[...]) + _dot(p_lo, ova_ref[...]))
        imp = share[0:Q_BLOCK]
        for hl in range(1, HEADS_PER_GROUP):
            imp = imp + share[hl * Q_BLOCK:(hl + 1) * Q_BLOCK]
        imp_t = imp.T[0:nb]
        blk = lax.broadcasted_iota(jnp.int32, (nb, Q_BLOCK), 0)
        cur = (t0 + lax.broadcasted_iota(jnp.int32, (nb, Q_BLOCK), 1)) // SEL_BLOCK
        forced = (blk == 0) | (blk == cur) | (blk == cur - 1)
        imp_a = jnp.where(forced, FORCE, jnp.where(blk <= cur, imp_t, -FORCE))
        tiles = [imp_a[r * SUBLANES:(r + 1) * SUBLANES] for r in range(nb // SUBLANES)]
        sub = lax.broadcasted_iota(jnp.int32, (SUBLANES, Q_BLOCK), 0)
        rank = [jnp.zeros((SUBLANES, Q_BLOCK), jnp.int32) for _ in tiles]
        for jp in range(nb):
            other = jnp.broadcast_to(imp_a[jp:jp + 1, :], (SUBLANES, Q_BLOCK))
            for r, tile in enumerate(tiles):
                if r > jp // SUBLANES:
                    before = other >= tile
                elif r < jp // SUBLANES:
                    before = other > tile
                else:
                    before = (other > tile) | ((other == tile) & (sub > jp % SUBLANES))
                rank[r] = rank[r] + before.astype(jnp.int32)
        neg_t = jnp.concatenate([jnp.where(r < n_sel, 0.0, NEG) for r in rank]
                                + [jnp.zeros((LANES - nb, Q_BLOCK), F32)], axis=0)
        lhs.append(jnp.concatenate([qs[g], rep(neg_t.T.astype(BF16))], axis=1))

    def scores(kt):
        k0 = pl.multiple_of(kt * KEY_TILE, KEY_TILE)
        rhs = jnp.concatenate([ksel_ref[0, pl.ds(k0, KEY_TILE), :], hot_ref[pl.ds(k0, KEY_TILE), :]], axis=1)
        return [_dot_nt(lhs[g], rhs) for g in groups]

    def values(kt, g):
        return vsel_refs[g][0, pl.ds(pl.multiple_of(kt * KEY_TILE, KEY_TILE), KEY_TILE), :]

    last = (t0 + Q_BLOCK - 1) // KEY_TILE
    k_last = last * KEY_TILE + lax.broadcasted_iota(jnp.int32, (1, KEY_TILE), 1)
    causal = rep(jnp.where(k_last <= t_col, 0.0, NEG))
    zeros = tuple(jnp.zeros((rows, LANES), F32) for _ in groups)

    if bounded:
        def sweep(kt, acc, mask=None):
            s_t = scores(kt)
            if mask is not None:
                s_t = [s + mask for s in s_t]
            return tuple(acc[g] + _dot(jnp.exp2(s_t[g]).astype(BF16), values(kt, g)) for g in groups)

        acc_s = sweep(last, _loop_pairs(0, last, sweep, zeros), causal)
    else:
        sbuf, = scratch

        def fold_max(m_run, s_t):
            return jnp.maximum(m_run, jnp.maximum(s_t[:, 0:LANES], s_t[:, LANES:KEY_TILE]))

        def pass1(kt, m_run):
            s_t = scores(kt)
            for g in groups:
                sbuf[g, kt] = s_t[g]
            return tuple(fold_max(m_run[g], s_t[g]) for g in groups)

        m_run = _loop_pairs(0, last, pass1, tuple(jnp.full((rows, LANES), NEG, F32) for _ in groups))
        s_t = scores(last)
        m_sel = []
        for g in groups:
            s_g = s_t[g] + causal
            sbuf[g, last] = s_g
            m_sel.append(jnp.broadcast_to(jnp.max(fold_max(m_run[g], s_g), axis=1, keepdims=True), (rows, LANES)))

        def pass2(kt, acc):
            out = []
            for g in groups:
                s_g = sbuf[g, kt]
                p = jnp.concatenate([jnp.exp2(s_g[:, 0:LANES] - m_sel[g]),
                                     jnp.exp2(s_g[:, LANES:KEY_TILE] - m_sel[g])], axis=1).astype(BF16)
                out.append(acc[g] + _dot(p, values(kt, g)))
            return tuple(out)

        acc_s = _loop_pairs(0, last + 1, pass2, zeros)

    for g in groups:
        o_s = normalized(acc_s[g])
        outs = []
        for hl in range(HEADS_PER_GROUP):
            h = g * HEADS_PER_GROUP + hl
            r = slice(hl * Q_BLOCK, (hl + 1) * Q_BLOCK)
            gcol = lambda br: gates[:, N_BRANCH * h + br:N_BRANCH * h + br + 1]
            outs.append(gcol(0) * o_c[g][r] + gcol(1) * o_s[r] + gcol(2) * o_w[g][r])
        for pair in range(HEADS_PER_GROUP // 2):
            left, right = outs[2 * pair], outs[2 * pair + 1]
            if g == 0:
                right = pltpu.roll(right, HEAD_DIM, axis=1)
            else:
                left = pltpu.roll(left, HEAD_DIM, axis=1)
            c0 = g * HEADS_PER_GROUP * HEAD_DIM + pair * LANES
            yb_ref[0, :, c0:c0 + LANES] = jnp.where(lane_q < HEAD_DIM, left, right).astype(BF16)


def _nsa_call(bounded, q, gates, kcmp, vcmp0, vcmp1, ksl, vsl0, vsl1, kwn, vwn0, vwn1, ova, hot):
    b, seq, _ = q.shape
    nc = seq // CMP_STRIDE
    wlen = min(WINDOW + Q_BLOCK, seq)
    n_sel = min(N_SEL, seq // SEL_BLOCK)
    qblk = lambda w: pl.BlockSpec((1, Q_BLOCK, w), lambda i, j: (i, j, 0))
    per_b = lambda rows: pl.BlockSpec((1, rows, KV_W), lambda i, j: (i, 0, 0))
    full = lambda a: pl.BlockSpec(a.shape, lambda i, j: (0,) * a.ndim)
    score_buf = pltpu.VMEM((N_GROUPS, seq // KEY_TILE, HEADS_PER_GROUP * Q_BLOCK, KEY_TILE), F32)
    return pl.pallas_call(
        functools.partial(_nsa_kernel, seq=seq, n_sel=n_sel, wlen=wlen, bounded=bounded),
        grid=(b, seq // Q_BLOCK),
        in_specs=[qblk(ATTN_W), qblk(GATE_PAD)] + [per_b(nc)] * 3 + [per_b(seq)] * 6 + [full(ova), full(hot)],
        out_specs=qblk(ATTN_W),
        out_shape=jax.ShapeDtypeStruct((b, seq, ATTN_W), BF16),
        scratch_shapes=[] if bounded else [score_buf],
        compiler_params=pltpu.CompilerParams(dimension_semantics=("arbitrary", "arbitrary"),
                                             vmem_limit_bytes=VMEM_LIMIT),
        name="nsa_bounded" if bounded else "nsa",
    )(q, gates, kcmp, vcmp0, vcmp1, ksl, vsl0, vsl1, kwn, vwn0, vwn1, ova, hot)


def _merge_kernel(x_ref, ya_ref, yb_ref, gm_ref, wa_ref, wb_ref, wo_ref, out_ref):
    gm = gm_ref[...].astype(F32)
    mix = gm[:, 0:D_MODEL] * _dot(ya_ref[...], wa_ref[...]) + gm[:, D_MODEL:] * _dot(yb_ref[...], wb_ref[...])
    out_ref[...] = x_ref[...] + _dot(mix.astype(BF16), wo_ref[...])


def _merge_call(x2, ya, yb, gm, wa, wb, wo, seq):
    n = x2.shape[0]
    tm = min(ROW_TILE, seq)
    row = lambda w: pl.BlockSpec((tm, w), lambda i: (i, 0))
    full = lambda a: pl.BlockSpec(a.shape, lambda i: (0,) * a.ndim)
    return pl.pallas_call(
        _merge_kernel,
        grid=(n // tm,),
        in_specs=[row(D_MODEL), row(CONV_W), row(ATTN_W), row(2 * D_MODEL), full(wa), full(wb), full(wo)],
        out_specs=row(D_MODEL),
        out_shape=jax.ShapeDtypeStruct((n, D_MODEL), F32),
        compiler_params=pltpu.CompilerParams(dimension_semantics=("arbitrary",), vmem_limit_bytes=VMEM_LIMIT),
        name="merge",
    )(x2, ya, yb, gm, wa, wb, wo)


_FF_CHUNKS = ((0, 1024), (1024, 1024), (2048, 768))


def _ffn_kernel(x_ref, g2_ref, win_ref, cw_ref, cb_ref, wout_ref, out_ref, gbuf, *, tiles_per_seq, tm):
    i = pl.program_id(0)

    @pl.when(i % tiles_per_seq == 0)
    def _():
        gbuf[0:SUBLANES, :] = jnp.zeros((SUBLANES, D_FF), F32)

    x = x_ref[...]
    ms = jnp.mean(x * x, axis=-1, keepdims=True)
    h = (x * lax.rsqrt(ms + EPS) * g2_ref[...]).astype(BF16)
    acc = jnp.zeros((tm, D_MODEL), F32)
    for c0, w in _FF_CHUNKS:
        cols = slice(c0, c0 + w)
        gpre = _dot(h, win_ref[:, c0:c0 + w])
        up = _dot(h, win_ref[:, D_FF + c0:D_FF + c0 + w])
        gbuf[SUBLANES:SUBLANES + tm, cols] = gpre
        cw = cw_ref[:, cols]
        conv = (cw[0:1, :] * gbuf[SUBLANES - 2:SUBLANES - 2 + tm, cols]
                + cw[1:2, :] * gbuf[SUBLANES - 1:SUBLANES - 1 + tm, cols]
                + cw[2:3, :] * gpre) + cb_ref[:, cols]
        gbuf[0:SUBLANES, cols] = gbuf[tm:tm + SUBLANES, cols]
        act = conv * jax.nn.sigmoid(conv) * up
        acc = acc + _dot(act.astype(BF16), wout_ref[c0:c0 + w, :])
    out_ref[...] = x + acc


def _ffn_call(x1, g2, win, conv_w, conv_b, wout, seq):
    n = x1.shape[0]
    tm = min(ROW_TILE, seq)
    row = lambda w: pl.BlockSpec((tm, w), lambda i: (i, 0))
    full = lambda a: pl.BlockSpec(a.shape, lambda i: (0,) * a.ndim)
    return pl.pallas_call(
        functools.partial(_ffn_kernel, tiles_per_seq=seq // tm, tm=tm),
        grid=(n // tm,),
        in_specs=[row(D_MODEL), full(g2), full(win), full(conv_w), full(conv_b), full(wout)],
        out_specs=row(D_MODEL),
        out_shape=jax.ShapeDtypeStruct((n, D_MODEL), F32),
        scratch_shapes=[pltpu.VMEM((tm + SUBLANES, D_FF), F32)],
        compiler_params=pltpu.CompilerParams(dimension_semantics=("arbitrary",), vmem_limit_bytes=VMEM_LIMIT),
        name="ffn",
    )(x1, g2, win, conv_w, conv_b, wout)


def _block_diag2(w):
    z = jnp.zeros_like(w)
    return jnp.concatenate([jnp.concatenate([w, z], axis=-1), jnp.concatenate([z, w], axis=-1)], axis=-2)


def _static_tables(seq):
    nc, nb = seq // CMP_STRIDE, seq // SEL_BLOCK
    n_cmp = (seq - CMP_BLOCK) // CMP_STRIDE + 1
    ii = np.arange(nc)[None, :] * CMP_STRIDE
    jj = np.arange(nb)[:, None] * SEL_BLOCK
    ovt = ((ii < jj + SEL_BLOCK) & (ii + CMP_BLOCK > jj) & (np.arange(nc)[None, :] < n_cmp)).astype(np.float32)
    ova = np.ones((nc, LANES), np.float32)
    ova[:, :HEAD_DIM] = 0.0
    ova[:, :nb] = ovt.T
    hot = (np.arange(LANES)[None, :] == (np.arange(seq) // SEL_BLOCK)[:, None]).astype(np.float32)
    return jnp.asarray(ova, BF16), jnp.asarray(hot, BF16)


def kernel(x, positions, norm1_g, w_in, conv_a_w, w_a_out, q_norm_g, k_norm_g, cmp_k_pe, cmp_k_w1, cmp_k_w2, cmp_v_pe, cmp_v_w1, cmp_v_w2, w_b_out, w_o, norm2_g, w_ffn_in, ffn_conv_w, ffn_conv_b, w_ffn_out):
    b, seq, d = x.shape
    n = b * seq
    half = HEAD_DIM // 2
    inv = ROPE_THETA ** (-jnp.arange(half, dtype=F32) / half)
    inv_lanes = jnp.tile(inv, LANES // half)[None, :]
    posf = positions.astype(F32).reshape(n, 1)
    ova, hot = _static_tables(seq)
    gate_cols = N_BRANCH * N_HEADS

    x2 = x.reshape(n, d)
    for l in range(w_in.shape[0]):
        w = w_in[l]
        w_pad = jnp.concatenate(
            [w[:, :_GATE_OFF + gate_cols], jnp.zeros((d, GATE_PAD - gate_cols), w.dtype), w[:, _GATE_OFF + gate_cols:]],
            axis=1).astype(BF16)
        qg = jnp.tile(q_norm_g[l], N_HEADS)[None, :]
        kg = jnp.tile(k_norm_g[l], (1, N_GROUPS))
        ya, q, kc, vc, ksl, vsl0, vsl1, kwn, vwn0, vwn1, gates, gm = _proj_call(
            x2, posf, norm1_g[l][None, :], w_pad, conv_a_w[l], qg, kg[1:3], inv_lanes, seq)

        per_b = lambda a: a.reshape(b, seq, a.shape[-1])
        w1_blocks = lambda w1: _block_diag2(w1.reshape(CMP_BLOCK, HEAD_DIM, HEAD_DIM)).astype(BF16)
        kcmp, vcmp0, vcmp1 = _compress_call(
            per_b(kc), per_b(vc), jnp.tile(cmp_k_pe[l], (1, N_GROUPS)), jnp.tile(cmp_v_pe[l], (1, N_GROUPS)),
            w1_blocks(cmp_k_w1[l]), _block_diag2(cmp_k_w2[l]).astype(BF16),
            w1_blocks(cmp_v_w1[l]), _block_diag2(cmp_v_w2[l]).astype(BF16), kg[0:1])

        reach = HEAD_DIM * (HEAD_DIM ** -0.5 * LOG2E) * jnp.max(jnp.abs(q_norm_g[l])) * jnp.max(jnp.abs(k_norm_g[l]))
        nsa_args = (per_b(q), per_b(gates), kcmp, vcmp0, vcmp1, per_b(ksl), per_b(vsl0), per_b(vsl1), per_b(kwn),
                    per_b(vwn0), per_b(vwn1), ova, hot)
        yb = lax.cond(reach * 1.02 <= SCORE_BOUND, functools.partial(_nsa_call, True),
                      functools.partial(_nsa_call, False), *nsa_args)

        x2 = _merge_call(x2, ya, yb.reshape(n, ATTN_W), gm, w_a_out[l].astype(BF16), w_b_out[l].astype(BF16),
                         w_o[l].astype(BF16), seq)
        x2 = _ffn_call(x2, norm2_g[l][None, :], w_ffn_in[l].astype(BF16), ffn_conv_w[l], ffn_conv_b[l][None, :],
                       w_ffn_out[l].astype(BF16), seq)
    return x2.reshape(b, seq, d)
```

```python
import functools

import jax
import jax.numpy as jnp
import numpy as np
from jax import lax
from jax.experimental import pallas as pl
from jax.experimental.pallas import tpu as pltpu

F32 = jnp.float32
BF16 = jnp.bfloat16

D_MODEL = 1024
CONV_W = 512
CONV_K = 3
N_HEADS = 8
HEAD_DIM = 64
N_GROUPS = 2
HEADS_PER_GROUP = N_HEADS // N_GROUPS
ATTN_W = N_HEADS * HEAD_DIM
KV_W = N_GROUPS * HEAD_DIM
CMP_BLOCK = 32
CMP_STRIDE = 16
SEL_BLOCK = 64
N_SEL = 16
WINDOW = 512
Q_BLOCK = 128
N_BRANCH = 3
ROPE_THETA = 10000.0
D_FF = 2816
EPS = 1e-6
NEG = -1e30
FORCE = 1e9
LOG2E = 1.4426950408889634
SCORE_BOUND = 60.0

LANES = 128
SUBLANES = 8
KEY_TILE = 256
SEG_W = 256
ROW_TILE = 512
GATE_PAD = LANES
VMEM_LIMIT = 56 * 1024 * 1024

_A_END = 3 * CONV_W
_Q_OFF = _A_END
_KV_OFF = _Q_OFF + ATTN_W
_GATE_OFF = _KV_OFF + 6 * KV_W
_GM_OFF = _GATE_OFF + GATE_PAD
_IN_PAD = _GM_OFF + 2 * D_MODEL


def _dot(a, b):
    return jnp.dot(a, b, preferred_element_type=F32)


def _dot_nt(a, b):
    return lax.dot_general(a, b, (((1,), (1,)), ((), ())), preferred_element_type=F32)


def _head_norm(x, gain, seg):
    sq = x * x
    hi = sq.astype(BF16)
    lo = (sq - hi.astype(F32)).astype(BF16)
    ms = (_dot(hi, seg) + _dot(lo, seg)) * (1.0 / HEAD_DIM)
    return x * lax.rsqrt(ms + EPS) * gain


def _rope(x, cos, sin_signed):
    lane = lax.broadcasted_iota(jnp.int32, x.shape, 1)
    half = HEAD_DIM // 2
    rot = jnp.where((lane & half) != 0, pltpu.roll(x, half, axis=1), pltpu.roll(x, LANES - half, axis=1))
    return x * cos + rot * sin_signed


def _proj_kernel(x_ref, pos_ref, g1_ref, w_ref, cw_ref, qg_ref, kg_ref, inv_ref, seg_ref,
                 ya_ref, q_ref, kc_ref, vc_ref, ksl_ref, vsl0_ref, vsl1_ref, kwn_ref, vwn0_ref, vwn1_ref, gate_ref,
                 gm_ref, ubuf, *, tiles_per_seq, tm):
    i = pl.program_id(0)

    @pl.when(i % tiles_per_seq == 0)
    def _():
        ubuf[0:SUBLANES, :] = jnp.zeros((SUBLANES, CONV_W), F32)

    x = x_ref[...]
    ms = jnp.mean(x * x, axis=-1, keepdims=True)
    h = (x * lax.rsqrt(ms + EPS) * g1_ref[...]).astype(BF16)

    pa = _dot(h, w_ref[:, 0:_A_END])
    u = pa[:, CONV_W:2 * CONV_W] * pa[:, 2 * CONV_W:3 * CONV_W]
    ubuf[SUBLANES:SUBLANES + tm, :] = u
    cw = cw_ref[...]
    y = (cw[0:1, :] * ubuf[SUBLANES - 2:SUBLANES - 2 + tm, :]
         + cw[1:2, :] * ubuf[SUBLANES - 1:SUBLANES - 1 + tm, :]
         + cw[2:3, :] * u)
    ya_ref[...] = (pa[:, 0:CONV_W] * y).astype(BF16)
    ubuf[0:SUBLANES, :] = ubuf[tm:tm + SUBLANES, :]

    ang = pos_ref[...] * inv_ref[...]
    lane = lax.broadcasted_iota(jnp.int32, ang.shape, 1)
    cos = jnp.cos(ang)
    sin_signed = jnp.where((lane & (HEAD_DIM // 2)) != 0, jnp.sin(ang), -jnp.sin(ang))

    pb = _dot(h, w_ref[:, _Q_OFF:_GM_OFF])
    scale = HEAD_DIM ** -0.5 * LOG2E
    seg = seg_ref[...]
    for c in range(ATTN_W // SEG_W):
        cols = slice(c * SEG_W, (c + 1) * SEG_W)
        qn = _head_norm(pb[:, cols], qg_ref[:, cols], seg)
        for t in range(SEG_W // LANES):
            lanes = slice(t * LANES, (t + 1) * LANES)
            q_ref[:, c * SEG_W + t * LANES:c * SEG_W + (t + 1) * LANES] = (
                _rope(qn[:, lanes], cos, sin_signed) * scale).astype(BF16)
    kv = pb[:, ATTN_W:ATTN_W + 6 * KV_W]
    kc_ref[...] = kv[:, 0:KV_W]
    vc_ref[...] = kv[:, KV_W:2 * KV_W]
    kn = _head_norm(jnp.concatenate([kv[:, 2 * KV_W:3 * KV_W], kv[:, 4 * KV_W:5 * KV_W]], axis=1), kg_ref[...], seg)
    ksl_ref[...] = _rope(kn[:, 0:KV_W], cos, sin_signed).astype(BF16)
    kwn_ref[...] = _rope(kn[:, KV_W:2 * KV_W], cos, sin_signed).astype(BF16)
    first = lane < HEAD_DIM
    for v, out0, out1 in ((kv[:, 3 * KV_W:4 * KV_W], vsl0_ref, vsl1_ref), (kv[:, 5 * KV_W:6 * KV_W], vwn0_ref, vwn1_ref)):
        out0[...] = jnp.where(first, v, 1.0).astype(BF16)
        out1[...] = jnp.where(first, 1.0, v).astype(BF16)
    gate_ref[...] = jax.nn.sigmoid(pb[:, ATTN_W + 6 * KV_W:])

    gm_ref[...] = jax.nn.sigmoid(_dot(h, w_ref[:, _GM_OFF:_IN_PAD])).astype(BF16)


def _proj_call(x2, posf, g1, w_pad, conv_w, qg, kg, inv, seg, seq):
    n = x2.shape[0]
    tm = min(ROW_TILE, seq)
    row = lambda w: pl.BlockSpec((tm, w), lambda i: (i, 0))
    full = lambda a: pl.BlockSpec(a.shape, lambda i: (0,) * a.ndim)
    out_shapes = (
        jax.ShapeDtypeStruct((n, CONV_W), BF16),
        jax.ShapeDtypeStruct((n, ATTN_W), BF16),
        jax.ShapeDtypeStruct((n, KV_W), F32),
        jax.ShapeDtypeStruct((n, KV_W), F32),
        jax.ShapeDtypeStruct((n, KV_W), BF16),
        jax.ShapeDtypeStruct((n, KV_W), BF16),
        jax.ShapeDtypeStruct((n, KV_W), BF16),
        jax.ShapeDtypeStruct((n, KV_W), BF16),
        jax.ShapeDtypeStruct((n, KV_W), BF16),
        jax.ShapeDtypeStruct((n, KV_W), BF16),
        jax.ShapeDtypeStruct((n, GATE_PAD), F32),
        jax.ShapeDtypeStruct((n, 2 * D_MODEL), BF16),
    )
    return pl.pallas_call(
        functools.partial(_proj_kernel, tiles_per_seq=seq // tm, tm=tm),
        grid=(n // tm,),
        in_specs=[row(D_MODEL), row(1), full(g1), full(w_pad), full(conv_w), full(qg), full(kg), full(inv), full(seg)],
        out_specs=[row(s.shape[1]) for s in out_shapes],
        out_shape=out_shapes,
        scratch_shapes=[pltpu.VMEM((tm + SUBLANES, CONV_W), F32)],
        compiler_params=pltpu.CompilerParams(dimension_semantics=("arbitrary",), vmem_limit_bytes=VMEM_LIMIT),
        name="proj",
    )(x2, posf, g1, w_pad, conv_w, qg, kg, inv, seg)


def _compress_kernel(kc_ref, vc_ref, pek_ref, pev_ref, w1k_ref, w2k_ref, w1v_ref, w2v_ref, kg_ref, seg_ref,
                     kcmp_ref, vcmp0_ref, vcmp1_ref, buf, *, seq):
    nc = seq // CMP_STRIDE
    buf[seq:seq + CMP_BLOCK, :] = jnp.zeros((CMP_BLOCK, KV_W), F32)
    for src, pe, w1, w2, is_key in ((kc_ref, pek_ref, w1k_ref, w2k_ref, True), (vc_ref, pev_ref, w1v_ref, w2v_ref, False)):
        buf[0:seq, :] = src[0]
        acc = jnp.zeros((nc, KV_W), F32)
        for r in range(CMP_BLOCK):
            xr = buf[pl.ds(r, nc, stride=CMP_STRIDE), :] + pe[r:r + 1, :]
            acc = acc + _dot(xr.astype(BF16), w1[r])
        hmid = acc * jax.nn.sigmoid(acc)
        o = _dot(hmid.astype(BF16), w2[...])
        if is_key:
            kcmp_ref[0] = _head_norm(o, kg_ref[...], seg_ref[0:KV_W, 0:KV_W]).astype(BF16)
        else:
            first = lax.broadcasted_iota(jnp.int32, o.shape, 1) < HEAD_DIM
            vcmp0_ref[0] = jnp.where(first, o, 1.0).astype(BF16)
            vcmp1_ref[0] = jnp.where(first, 1.0, o).astype(BF16)


def _compress_call(kc, vc, pek, pev, w1k, w2k, w1v, w2v, kg0, seg):
    b, seq, _ = kc.shape
    nc = seq // CMP_STRIDE
    per_b = lambda rows: pl.BlockSpec((1, rows, KV_W), lambda i: (i, 0, 0))
    full = lambda a: pl.BlockSpec(a.shape, lambda i: (0,) * a.ndim)
    return pl.pallas_call(
        functools.partial(_compress_kernel, seq=seq),
        grid=(b,),
        in_specs=[per_b(seq), per_b(seq), full(pek), full(pev), full(w1k), full(w2k), full(w1v), full(w2v), full(kg0),
                  full(seg)],
        out_specs=[per_b(nc)] * 3,
        out_shape=(jax.ShapeDtypeStruct((b, nc, KV_W), BF16),) * 3,
        scratch_shapes=[pltpu.VMEM((seq + CMP_BLOCK, KV_W), F32)],
        compiler_params=pltpu.CompilerParams(dimension_semantics=("arbitrary",), vmem_limit_bytes=VMEM_LIMIT),
        name="compress",
    )(kc, vc, pek, pev, w1k, w2k, w1v, w2v, kg0, seg)


def _loop_pairs(lo, hi, step, carry):
    pairs = (hi - lo) // 2

    def two(i, c):
        return step(lo + 2 * i + 1, step(lo + 2 * i, c))

    return lax.fori_loop(lo + 2 * pairs, hi, step, lax.fori_loop(0, pairs, two, carry))


def _nsa_kernel(q_ref, gate_ref, kcmp_ref, vcmp0_ref, vcmp1_ref, ksel_ref, vsel0_ref, vsel1_ref, kwin_ref, vwin0_ref,
                vwin1_ref, ova_ref, hot_ref, yb_ref, *scratch, seq, n_sel, wlen, bounded):
    qb = pl.program_id(1)
    t0 = qb * Q_BLOCK
    nc = seq // CMP_STRIDE
    nb = seq // SEL_BLOCK
    rows = HEADS_PER_GROUP * Q_BLOCK
    groups = range(N_GROUPS)
    vcmp_refs = (vcmp0_ref, vcmp1_ref)
    vsel_refs = (vsel0_ref, vsel1_ref)
    vwin_refs = (vwin0_ref, vwin1_ref)

    q = q_ref[0]
    gates = gate_ref[0]
    lane_q = lax.broadcasted_iota(jnp.int32, (Q_BLOCK, LANES), 1)
    t_col = t0 + lax.broadcasted_iota(jnp.int32, (Q_BLOCK, 1), 0)
    rep = lambda a: jnp.concatenate([a] * HEADS_PER_GROUP, axis=0)

    def weights(s):
        if bounded:
            return jnp.exp2(s)
        m = jnp.maximum(jnp.max(s, axis=1, keepdims=True), 0.5 * NEG)
        return jnp.exp2(s - m)

    def normalized(acc):
        l = pltpu.roll(acc, HEAD_DIM, axis=1)
        return acc * jnp.where(l > 0.0, 1.0 / l, 0.0)

    start = pl.multiple_of(jnp.maximum(t0 + Q_BLOCK - wlen, 0), Q_BLOCK)
    diff = t_col - (start + lax.broadcasted_iota(jnp.int32, (1, wlen), 1))
    band = rep(jnp.where((diff >= 0) & (diff < WINDOW), 0.0, NEG))
    kw = kwin_ref[0, pl.ds(start, wlen), :]
    cend = lax.broadcasted_iota(jnp.int32, (1, nc), 1) * CMP_STRIDE + (CMP_BLOCK - 1)
    done = rep(jnp.where(cend <= t_col, 0.0, NEG))

    qs, o_c, o_w, lhs = [], [], [], []
    for g in groups:
        in_group = (lane_q >= HEAD_DIM * g) & (lane_q < HEAD_DIM * (g + 1))
        parts = []
        for hl in range(HEADS_PER_GROUP):
            pair, side = divmod(hl, 2)
            c0 = g * HEADS_PER_GROUP * HEAD_DIM + pair * LANES
            qp = q[:, c0:c0 + LANES].astype(F32)
            if side != g:
                qp = pltpu.roll(qp, HEAD_DIM, axis=1)
            parts.append(jnp.where(in_group, qp, 0.0).astype(BF16))
        qs.append(jnp.concatenate(parts, axis=0))

        p = weights(_dot_nt(qs[g], kw) + band)
        o_w.append(normalized(_dot(p.astype(BF16), vwin_refs[g][0, pl.ds(start, wlen), :])))

        p = weights(_dot_nt(qs[g], kcmp_ref[0]) + done)
        both = _dot(p.astype(BF16), jnp.concatenate([vcmp_refs[g][0], ova_ref[...]], axis=1))
        o_c.append(normalized(both[:, 0:LANES]))

        share = normalized(both[:, LANES:2 * LANES])
        imp = share[0:Q_BLOCK]
        for hl in range(1, HEADS_PER_GROUP):
            imp = imp + share[hl * Q_BLOCK:(hl + 1) * Q_BLOCK]
        imp_t = imp.T[0:nb]
        blk = lax.broadcasted_iota(jnp.int32, (nb, Q_BLOCK), 0)
        cur = (t0 + lax.broadcasted_iota(jnp.int32, (nb, Q_BLOCK), 1)) // SEL_BLOCK
        forced = (blk == 0) | (blk == cur) | (blk == cur - 1)
        imp_a = jnp.where(forced, FORCE, jnp.where(blk <= cur, imp_t, -FORCE))
        tiles = [imp_a[r * SUBLANES:(r + 1) * SUBLANES] for r in range(nb // SUBLANES)]
        sub = lax.broadcasted_iota(jnp.int32, (SUBLANES, Q_BLOCK), 0)
        rank = [jnp.zeros((SUBLANES, Q_BLOCK), jnp.int32) for _ in tiles]
        for jp in range(nb):
            other = jnp.broadcast_to(imp_a[jp:jp + 1, :], (SUBLANES, Q_BLOCK))
            for r, tile in enumerate(tiles):
                if r > jp // SUBLANES:
                    before = other >= tile
                elif r < jp // SUBLANES:
                    before = other > tile
                else:
                    before = (other > tile) | ((other == tile) & (sub > jp % SUBLANES))
                rank[r] = rank[r] + before.astype(jnp.int32)
        neg_t = jnp.concatenate([jnp.where(r < n_sel, 0.0, NEG) for r in rank]
                                + [jnp.zeros((LANES - nb, Q_BLOCK), F32)], axis=0)
        lhs.append(jnp.concatenate([qs[g], rep(neg_t.T.astype(BF16))], axis=1))

    def scores(kt):
        k0 = pl.multiple_of(kt * KEY_TILE, KEY_TILE)
        rhs = jnp.concatenate([ksel_ref[0, pl.ds(k0, KEY_TILE), :], hot_ref[pl.ds(k0, KEY_TILE), :]], axis=1)
        kpos = k0 + lax.broadcasted_iota(jnp.int32, (1, KEY_TILE), 1)
        causal = rep(jnp.where(kpos <= t_col, 0.0, NEG))
        return [_dot_nt(lhs[g], rhs) + causal for g in groups]

    def values(kt, g):
        return vsel_refs[g][0, pl.ds(pl.multiple_of(kt * KEY_TILE, KEY_TILE), KEY_TILE), :]

    n_tiles = (t0 + Q_BLOCK - 1) // KEY_TILE + 1
    zeros = tuple(jnp.zeros((rows, LANES), F32) for _ in groups)

    if bounded:
        def sweep(kt, acc):
            s_t = scores(kt)
            return tuple(acc[g] + _dot(jnp.exp2(s_t[g]).astype(BF16), values(kt, g)) for g in groups)

        acc_s = _loop_pairs(0, n_tiles, sweep, zeros)
    else:
        sbuf, = scratch

        def fold_max(m_run, s_t):
            return jnp.maximum(m_run, jnp.maximum(s_t[:, 0:LANES], s_t[:, LANES:KEY_TILE]))

        def pass1(kt, m_run):
            s_t = scores(kt)
            for g in groups:
                sbuf[g, kt] = s_t[g]
            return tuple(fold_max(m_run[g], s_t[g]) for g in groups)

        m_run = _loop_pairs(0, n_tiles, pass1, tuple(jnp.full((rows, LANES), NEG, F32) for _ in groups))
        m_sel = [jnp.broadcast_to(jnp.max(m_run[g], axis=1, keepdims=True), (rows, LANES)) for g in groups]

        def pass2(kt, acc):
            out = []
            for g in groups:
                s_g = sbuf[g, kt]
                p = jnp.concatenate([jnp.exp2(s_g[:, 0:LANES] - m_sel[g]),
                                     jnp.exp2(s_g[:, LANES:KEY_TILE] - m_sel[g])], axis=1).astype(BF16)
                out.append(acc[g] + _dot(p, values(kt, g)))
            return tuple(out)

        acc_s = _loop_pairs(0, n_tiles, pass2, zeros)

    for g in groups:
        o_s = normalized(acc_s[g])
        outs = []
        for hl in range(HEADS_PER_GROUP):
            h = g * HEADS_PER_GROUP + hl
            r = slice(hl * Q_BLOCK, (hl + 1) * Q_BLOCK)
            gcol = lambda br: gates[:, N_BRANCH * h + br:N_BRANCH * h + br + 1]
            outs.append(gcol(0) * o_c[g][r] + gcol(1) * o_s[r] + gcol(2) * o_w[g][r])
        for pair in range(HEADS_PER_GROUP // 2):
            left, right = outs[2 * pair], outs[2 * pair + 1]
            if g == 0:
                right = pltpu.roll(right, HEAD_DIM, axis=1)
            else:
                left = pltpu.roll(left, HEAD_DIM, axis=1)
            c0 = g * HEADS_PER_GROUP * HEAD_DIM + pair * LANES
            yb_ref[0, :, c0:c0 + LANES] = jnp.where(lane_q < HEAD_DIM, left, right).astype(BF16)


def _nsa_call(bounded, q, gates, kcmp, vcmp0, vcmp1, ksl, vsl0, vsl1, kwn, vwn0, vwn1, ova, hot):
    b, seq, _ = q.shape
    nc = seq // CMP_STRIDE
    wlen = min(WINDOW + Q_BLOCK, seq)
    n_sel = min(N_SEL, seq // SEL_BLOCK)
    qblk = lambda w: pl.BlockSpec((1, Q_BLOCK, w), lambda i, j: (i, j, 0))
    per_b = lambda rows: pl.BlockSpec((1, rows, KV_W), lambda i, j: (i, 0, 0))
    full = lambda a: pl.BlockSpec(a.shape, lambda i, j: (0,) * a.ndim)
    score_buf = pltpu.VMEM((N_GROUPS, seq // KEY_TILE, HEADS_PER_GROUP * Q_BLOCK, KEY_TILE), F32)
    return pl.pallas_call(
        functools.partial(_nsa_kernel, seq=seq, n_sel=n_sel, wlen=wlen, bounded=bounded),
        grid=(b, seq // Q_BLOCK),
        in_specs=[qblk(ATTN_W), qblk(GATE_PAD)] + [per_b(nc)] * 3 + [per_b(seq)] * 6 + [full(ova), full(hot)],
        out_specs=qblk(ATTN_W),
        out_shape=jax.ShapeDtypeStruct((b, seq, ATTN_W), BF16),
        scratch_shapes=[] if bounded else [score_buf],
        compiler_params=pltpu.CompilerParams(dimension_semantics=("arbitrary", "arbitrary"),
                                             vmem_limit_bytes=VMEM_LIMIT),
        name="nsa_bounded" if bounded else "nsa",
    )(q, gates, kcmp, vcmp0, vcmp1, ksl, vsl0, vsl1, kwn, vwn0, vwn1, ova, hot)


def _merge_kernel(x_ref, ya_ref, yb_ref, gm_ref, wa_ref, wb_ref, wo_ref, out_ref):
    gm = gm_ref[...].astype(F32)
    mix = gm[:, 0:D_MODEL] * _dot(ya_ref[...], wa_ref[...]) + gm[:, D_MODEL:] * _dot(yb_ref[...], wb_ref[...])
    out_ref[...] = x_ref[...] + _dot(mix.astype(BF16), wo_ref[...])


def _merge_call(x2, ya, yb, gm, wa, wb, wo, seq):
    n = x2.shape[0]
    tm = min(ROW_TILE, seq)
    row = lambda w: pl.BlockSpec((tm, w), lambda i: (i, 0))
    full = lambda a: pl.BlockSpec(a.shape, lambda i: (0,) * a.ndim)
    return pl.pallas_call(
        _merge_kernel,
        grid=(n // tm,),
        in_specs=[row(D_MODEL), row(CONV_W), row(ATTN_W), row(2 * D_MODEL), full(wa), full(wb), full(wo)],
        out_specs=row(D_MODEL),
        out_shape=jax.ShapeDtypeStruct((n, D_MODEL), F32),
        compiler_params=pltpu.CompilerParams(dimension_semantics=("arbitrary",), vmem_limit_bytes=VMEM_LIMIT),
        name="merge",
    )(x2, ya, yb, gm, wa, wb, wo)


_FF_CHUNKS = ((0, 1024), (1024, 1024), (2048, 768))


def _ffn_kernel(x_ref, g2_ref, win_ref, cw_ref, cb_ref, wout_ref, out_ref, gbuf, *, tiles_per_seq, tm):
    i = pl.program_id(0)

    @pl.when(i % tiles_per_seq == 0)
    def _():
        gbuf[0:SUBLANES, :] = jnp.zeros((SUBLANES, D_FF), F32)

    x = x_ref[...]
    ms = jnp.mean(x * x, axis=-1, keepdims=True)
    h = (x * lax.rsqrt(ms + EPS) * g2_ref[...]).astype(BF16)
    acc = jnp.zeros((tm, D_MODEL), F32)
    for c0, w in _FF_CHUNKS:
        cols = slice(c0, c0 + w)
        gpre = _dot(h, win_ref[:, c0:c0 + w])
        up = _dot(h, win_ref[:, D_FF + c0:D_FF + c0 + w])
        gbuf[SUBLANES:SUBLANES + tm, cols] = gpre
        cw = cw_ref[:, cols]
        conv = (cw[0:1, :] * gbuf[SUBLANES - 2:SUBLANES - 2 + tm, cols]
                + cw[1:2, :] * gbuf[SUBLANES - 1:SUBLANES - 1 + tm, cols]
                + cw[2:3, :] * gpre) + cb_ref[:, cols]
        gbuf[0:SUBLANES, cols] = gbuf[tm:tm + SUBLANES, cols]
        act = conv * jax.nn.sigmoid(conv) * up
        acc = acc + _dot(act.astype(BF16), wout_ref[c0:c0 + w, :])
    out_ref[...] = x + acc


def _ffn_call(x1, g2, win, conv_w, conv_b, wout, seq):
    n = x1.shape[0]
    tm = min(ROW_TILE, seq)
    row = lambda w: pl.BlockSpec((tm, w), lambda i: (i, 0))
    full = lambda a: pl.BlockSpec(a.shape, lambda i: (0,) * a.ndim)
    return pl.pallas_call(
        functools.partial(_ffn_kernel, tiles_per_seq=seq // tm, tm=tm),
        grid=(n // tm,),
        in_specs=[row(D_MODEL), full(g2), full(win), full(conv_w), full(conv_b), full(wout)],
        out_specs=row(D_MODEL),
        out_shape=jax.ShapeDtypeStruct((n, D_MODEL), F32),
        scratch_shapes=[pltpu.VMEM((tm + SUBLANES, D_FF), F32)],
        compiler_params=pltpu.CompilerParams(dimension_semantics=("arbitrary",), vmem_limit_bytes=VMEM_LIMIT),
        name="ffn",
    )(x1, g2, win, conv_w, conv_b, wout)


def _block_diag2(w):
    z = jnp.zeros_like(w)
    return jnp.concatenate([jnp.concatenate([w, z], axis=-1), jnp.concatenate([z, w], axis=-1)], axis=-2)


def _static_tables(seq):
    nc, nb = seq // CMP_STRIDE, seq // SEL_BLOCK
    n_cmp = (seq - CMP_BLOCK) // CMP_STRIDE + 1
    ii = np.arange(nc)[None, :] * CMP_STRIDE
    jj = np.arange(nb)[:, None] * SEL_BLOCK
    ovt = ((ii < jj + SEL_BLOCK) & (ii + CMP_BLOCK > jj) & (np.arange(nc)[None, :] < n_cmp)).astype(np.float32)
    ova = np.ones((nc, LANES), np.float32)
    ova[:, :HEAD_DIM] = 0.0
    ova[:, :nb] = ovt.T
    hot = (np.arange(LANES)[None, :] == (np.arange(seq) // SEL_BLOCK)[:, None]).astype(np.float32)
    head = np.arange(SEG_W) // HEAD_DIM
    seg = (head[:, None] == head[None, :]).astype(np.float32)
    return jnp.asarray(ova, BF16), jnp.asarray(hot, BF16), jnp.asarray(seg, BF16)


def kernel(x, positions, norm1_g, w_in, conv_a_w, w_a_out, q_norm_g, k_norm_g, cmp_k_pe, cmp_k_w1, cmp_k_w2, cmp_v_pe, cmp_v_w1, cmp_v_w2, w_b_out, w_o, norm2_g, w_ffn_in, ffn_conv_w, ffn_conv_b, w_ffn_out):
    b, seq, d = x.shape
    n = b * seq
    half = HEAD_DIM // 2
    inv = ROPE_THETA ** (-jnp.arange(half, dtype=F32) / half)
    inv_lanes = jnp.tile(inv, LANES // half)[None, :]
    posf = positions.astype(F32).reshape(n, 1)
    ova, hot, seg = _static_tables(seq)
    gate_cols = N_BRANCH * N_HEADS

    x2 = x.reshape(n, d)
    for l in range(w_in.shape[0]):
        w = w_in[l]
        w_pad = jnp.concatenate(
            [w[:, :_GATE_OFF + gate_cols], jnp.zeros((d, GATE_PAD - gate_cols), w.dtype), w[:, _GATE_OFF + gate_cols:]],
            axis=1).astype(BF16)
        qg = jnp.tile(q_norm_g[l], N_HEADS)[None, :]
        kg = jnp.tile(k_norm_g[l], (1, N_GROUPS))
        ya, q, kc, vc, ksl, vsl0, vsl1, kwn, vwn0, vwn1, gates, gm = _proj_call(
            x2, posf, norm1_g[l][None, :], w_pad, conv_a_w[l], qg, kg[1:3].reshape(1, 2 * KV_W), inv_lanes, seg, seq)

        per_b = lambda a: a.reshape(b, seq, a.shape[-1])
        w1_blocks = lambda w1: _block_diag2(w1.reshape(CMP_BLOCK, HEAD_DIM, HEAD_DIM)).astype(BF16)
        kcmp, vcmp0, vcmp1 = _compress_call(
            per_b(kc), per_b(vc), jnp.tile(cmp_k_pe[l], (1, N_GROUPS)), jnp.tile(cmp_v_pe[l], (1, N_GROUPS)),
            w1_blocks(cmp_k_w1[l]), _block_diag2(cmp_k_w2[l]).astype(BF16),
            w1_blocks(cmp_v_w1[l]), _block_diag2(cmp_v_w2[l]).astype(BF16), kg[0:1], seg)

        reach = HEAD_DIM * (HEAD_DIM ** -0.5 * LOG2E) * jnp.max(jnp.abs(q_norm_g[l])) * jnp.max(jnp.abs(k_norm_g[l]))
        nsa_args = (per_b(q), per_b(gates), kcmp, vcmp0, vcmp1, per_b(ksl), per_b(vsl0), per_b(vsl1), per_b(kwn),
                    per_b(vwn0), per_b(vwn1), ova, hot)
        yb = lax.cond(reach * 1.02 <= SCORE_BOUND, functools.partial(_nsa_call, True),
                      functools.partial(_nsa_call, False), *nsa_args)

        x2 = _merge_call(x2, ya, yb.reshape(n, ATTN_W), gm, w_a_out[l].astype(BF16), w_b_out[l].astype(BF16),
                         w_o[l].astype(BF16), seq)
        x2 = _ffn_call(x2, norm2_g[l][None, :], w_ffn_in[l].astype(BF16), ffn_conv_w[l], ffn_conv_b[l][None, :],
                       w_ffn_out[l].astype(BF16), seq)
    return x2.reshape(b, seq, d)
```

```python
import functools

import jax
import jax.numpy as jnp
import numpy as np
from jax import lax
from jax.experimental import pallas as pl
from jax.experimental.pallas import tpu as pltpu

F32 = jnp.float32
BF16 = jnp.bfloat16

D_MODEL = 1024
CONV_W = 512
CONV_K = 3
N_HEADS = 8
HEAD_DIM = 64
N_GROUPS = 2
HEADS_PER_GROUP = N_HEADS // N_GROUPS
ATTN_W = N_HEADS * HEAD_DIM
KV_W = N_GROUPS * HEAD_DIM
CMP_BLOCK = 32
CMP_STRIDE = 16
SEL_BLOCK = 64
N_SEL = 16
WINDOW = 512
Q_BLOCK = 128
N_BRANCH = 3
ROPE_THETA = 10000.0
D_FF = 2816
EPS = 1e-6
NEG = -1e30
FORCE = 1e9
LOG2E = 1.4426950408889634
SCORE_BOUND = 60.0

LANES = 128
SUBLANES = 8
KEY_TILE = 512
SEG_W = 256
ROW_TILE = 512
GATE_PAD = LANES
VMEM_LIMIT = 56 * 1024 * 1024

_A_END = 3 * CONV_W
_Q_OFF = _A_END
_KV_OFF = _Q_OFF + ATTN_W
_GATE_OFF = _KV_OFF + 6 * KV_W


def _dot(a, b):
    return jnp.dot(a, b, preferred_element_type=F32)


def _dot_nt(a, b):
    return lax.dot_general(a, b, (((1,), (1,)), ((), ())), preferred_element_type=F32)


def _head_norm(x, gain, seg):
    sq = x * x
    hi = sq.astype(BF16)
    lo = (sq - hi.astype(F32)).astype(BF16)
    ms = (_dot(hi, seg) + _dot(lo, seg)) * (1.0 / HEAD_DIM)
    return x * lax.rsqrt(ms + EPS) * gain


def _rope(x, cos, sin_signed):
    lane = lax.broadcasted_iota(jnp.int32, x.shape, 1)
    half = HEAD_DIM // 2
    rot = jnp.where((lane & half) != 0, pltpu.roll(x, half, axis=1), pltpu.roll(x, LANES - half, axis=1))
    return x * cos + rot * sin_signed


def _proj_kernel(x_ref, pos_ref, g1_ref, w_ref, wgate_ref, wgm_ref, cw_ref, qg_ref, kg_ref, inv_ref, seg_ref,
                 ya_ref, q_ref, kc_ref, vc_ref, ksl_ref, vsl0_ref, vsl1_ref, kwn_ref, vwn0_ref, vwn1_ref, gate_ref,
                 gm_ref, ubuf, *, tiles_per_seq, tm):
    i = pl.program_id(0)

    @pl.when(i % tiles_per_seq == 0)
    def _():
        ubuf[0:SUBLANES, :] = jnp.zeros((SUBLANES, CONV_W), F32)

    x = x_ref[...]
    ms = jnp.mean(x * x, axis=-1, keepdims=True)
    h = (x * lax.rsqrt(ms + EPS) * g1_ref[...]).astype(BF16)

    pa = _dot(h, w_ref[:, 0:_A_END])
    u = pa[:, CONV_W:2 * CONV_W] * pa[:, 2 * CONV_W:3 * CONV_W]
    ubuf[SUBLANES:SUBLANES + tm, :] = u
    cw = cw_ref[...]
    y = (cw[0:1, :] * ubuf[SUBLANES - 2:SUBLANES - 2 + tm, :]
         + cw[1:2, :] * ubuf[SUBLANES - 1:SUBLANES - 1 + tm, :]
         + cw[2:3, :] * u)
    ya_ref[...] = (pa[:, 0:CONV_W] * y).astype(BF16)
    ubuf[0:SUBLANES, :] = ubuf[tm:tm + SUBLANES, :]

    ang = pos_ref[...] * inv_ref[...]
    lane = lax.broadcasted_iota(jnp.int32, ang.shape, 1)
    cos = jnp.cos(ang)
    sin_signed = jnp.where((lane & (HEAD_DIM // 2)) != 0, jnp.sin(ang), -jnp.sin(ang))

    pb = _dot(h, w_ref[:, _Q_OFF:_GATE_OFF])
    scale = HEAD_DIM ** -0.5 * LOG2E
    seg = seg_ref[...]
    for c in range(ATTN_W // SEG_W):
        cols = slice(c * SEG_W, (c + 1) * SEG_W)
        qn = _head_norm(pb[:, cols], qg_ref[:, cols], seg)
        for t in range(SEG_W // LANES):
            lanes = slice(t * LANES, (t + 1) * LANES)
            q_ref[:, c * SEG_W + t * LANES:c * SEG_W + (t + 1) * LANES] = (
                _rope(qn[:, lanes], cos, sin_signed) * scale).astype(BF16)
    kv = pb[:, ATTN_W:ATTN_W + 6 * KV_W]
    kc_ref[...] = kv[:, 0:KV_W]
    vc_ref[...] = kv[:, KV_W:2 * KV_W]
    kn = _head_norm(jnp.concatenate([kv[:, 2 * KV_W:3 * KV_W], kv[:, 4 * KV_W:5 * KV_W]], axis=1), kg_ref[...], seg)
    ksl_ref[...] = _rope(kn[:, 0:KV_W], cos, sin_signed).astype(BF16)
    kwn_ref[...] = _rope(kn[:, KV_W:2 * KV_W], cos, sin_signed).astype(BF16)
    first = lane < HEAD_DIM
    for v, out0, out1 in ((kv[:, 3 * KV_W:4 * KV_W], vsl0_ref, vsl1_ref), (kv[:, 5 * KV_W:6 * KV_W], vwn0_ref, vwn1_ref)):
        out0[...] = jnp.where(first, v, 1.0).astype(BF16)
        out1[...] = jnp.where(first, 1.0, v).astype(BF16)
    gate_ref[...] = jax.nn.sigmoid(_dot(h, wgate_ref[...]))

    gm_ref[...] = jax.nn.sigmoid(_dot(h, wgm_ref[...])).astype(BF16)


def _proj_call(x2, posf, g1, w_main, w_gate, w_gm, conv_w, qg, kg, inv, seg, seq):
    n = x2.shape[0]
    tm = min(ROW_TILE, seq)
    row = lambda w: pl.BlockSpec((tm, w), lambda i: (i, 0))
    full = lambda a: pl.BlockSpec(a.shape, lambda i: (0,) * a.ndim)
    once = lambda a: pl.BlockSpec(a.shape, lambda i: (0,) * a.ndim, pipeline_mode=pl.Buffered(1))
    out_shapes = (
        jax.ShapeDtypeStruct((n, CONV_W), BF16),
        jax.ShapeDtypeStruct((n, ATTN_W), BF16),
        jax.ShapeDtypeStruct((n, KV_W), F32),
        jax.ShapeDtypeStruct((n, KV_W), F32),
        jax.ShapeDtypeStruct((n, KV_W), BF16),
        jax.ShapeDtypeStruct((n, KV_W), BF16),
        jax.ShapeDtypeStruct((n, KV_W), BF16),
        jax.ShapeDtypeStruct((n, KV_W), BF16),
        jax.ShapeDtypeStruct((n, KV_W), BF16),
        jax.ShapeDtypeStruct((n, KV_W), BF16),
        jax.ShapeDtypeStruct((n, GATE_PAD), F32),
        jax.ShapeDtypeStruct((n, 2 * D_MODEL), BF16),
    )
    return pl.pallas_call(
        functools.partial(_proj_kernel, tiles_per_seq=seq // tm, tm=tm),
        grid=(n // tm,),
        in_specs=[row(D_MODEL), row(1), full(g1), once(w_main), once(w_gate), once(w_gm), full(conv_w), full(qg),
                  full(kg), full(inv), full(seg)],
        out_specs=[row(s.shape[1]) for s in out_shapes],
        out_shape=out_shapes,
        scratch_shapes=[pltpu.VMEM((tm + SUBLANES, CONV_W), F32)],
        compiler_params=pltpu.CompilerParams(dimension_semantics=("arbitrary",), vmem_limit_bytes=VMEM_LIMIT),
        name="proj",
    )(x2, posf, g1, w_main, w_gate, w_gm, conv_w, qg, kg, inv, seg)


def _compress_kernel(kc_ref, vc_ref, pek_ref, pev_ref, w1k_ref, w2k_ref, w1v_ref, w2v_ref, kg_ref, seg_ref,
                     kcmp_ref, vcmp0_ref, vcmp1_ref, buf, *, seq):
    nc = seq // CMP_STRIDE
    buf[seq:seq + CMP_BLOCK, :] = jnp.zeros((CMP_BLOCK, KV_W), F32)
    for src, pe, w1, w2, is_key in ((kc_ref, pek_ref, w1k_ref, w2k_ref, True), (vc_ref, pev_ref, w1v_ref, w2v_ref, False)):
        buf[0:seq, :] = src[0]
        acc = jnp.zeros((nc, KV_W), F32)
        for r in range(CMP_BLOCK):
            xr = buf[pl.ds(r, nc, stride=CMP_STRIDE), :] + pe[r:r + 1, :]
            acc = acc + _dot(xr.astype(BF16), w1[r])
        hmid = acc * jax.nn.sigmoid(acc)
        o = _dot(hmid.astype(BF16), w2[...])
        if is_key:
            kcmp_ref[0] = _head_norm(o, kg_ref[...], seg_ref[0:KV_W, 0:KV_W]).astype(BF16)
        else:
            first = lax.broadcasted_iota(jnp.int32, o.shape, 1) < HEAD_DIM
            vcmp0_ref[0] = jnp.where(first, o, 1.0).astype(BF16)
            vcmp1_ref[0] = jnp.where(first, 1.0, o).astype(BF16)


def _compress_call(kc, vc, pek, pev, w1k, w2k, w1v, w2v, kg0, seg):
    b, seq, _ = kc.shape
    nc = seq // CMP_STRIDE
    per_b = lambda rows: pl.BlockSpec((1, rows, KV_W), lambda i: (i, 0, 0))
    full = lambda a: pl.BlockSpec(a.shape, lambda i: (0,) * a.ndim)
    return pl.pallas_call(
        functools.partial(_compress_kernel, seq=seq),
        grid=(b,),
        in_specs=[per_b(seq), per_b(seq), full(pek), full(pev), full(w1k), full(w2k), full(w1v), full(w2v), full(kg0),
                  full(seg)],
        out_specs=[per_b(nc)] * 3,
        out_shape=(jax.ShapeDtypeStruct((b, nc, KV_W), BF16),) * 3,
        scratch_shapes=[pltpu.VMEM((seq + CMP_BLOCK, KV_W), F32)],
        compiler_params=pltpu.CompilerParams(dimension_semantics=("arbitrary",), vmem_limit_bytes=VMEM_LIMIT),
        name="compress",
    )(kc, vc, pek, pev, w1k, w2k, w1v, w2v, kg0, seg)


def _loop_pairs(lo, hi, step, carry):
    pairs = (hi - lo) // 2

    def two(i, c):
        return step(lo + 2 * i + 1, step(lo + 2 * i, c))

    return lax.fori_loop(lo + 2 * pairs, hi, step, lax.fori_loop(0, pairs, two, carry))


def _nsa_kernel(q_ref, gate_ref, kcmp_ref, vcmp0_ref, vcmp1_ref, ksel_ref, vsel0_ref, vsel1_ref, kwin_ref, vwin0_ref,
                vwin1_ref, ova_ref, hot_ref, yb_ref, *scratch, seq, n_sel, wlen, bounded):
    qb = pl.program_id(1)
    t0 = qb * Q_BLOCK
    nc = seq // CMP_STRIDE
    nb = seq // SEL_BLOCK
    rows = HEADS_PER_GROUP * Q_BLOCK
    groups = range(N_GROUPS)
    vcmp_refs = (vcmp0_ref, vcmp1_ref)
    vsel_refs = (vsel0_ref, vsel1_ref)
    vwin_refs = (vwin0_ref, vwin1_ref)

    q = q_ref[0]
    gates = gate_ref[0]
    lane_q = lax.broadcasted_iota(jnp.int32, (Q_BLOCK, LANES), 1)
    t_col = t0 + lax.broadcasted_iota(jnp.int32, (Q_BLOCK, 1), 0)
    rep = lambda a: jnp.concatenate([a] * HEADS_PER_GROUP, axis=0)

    def weights(s):
        if bounded:
            return jnp.exp2(s)
        m = jnp.maximum(jnp.max(s, axis=1, keepdims=True), 0.5 * NEG)
        return jnp.exp2(s - m)

    def normalized(acc):
        l = pltpu.roll(acc, HEAD_DIM, axis=1)
        return acc * jnp.where(l > 0.0, 1.0 / l, 0.0)

    start = pl.multiple_of(jnp.maximum(t0 + Q_BLOCK - wlen, 0), Q_BLOCK)
    diff = t_col - (start + lax.broadcasted_iota(jnp.int32, (1, wlen), 1))
    band = rep(jnp.where((diff >= 0) & (diff < WINDOW), 0.0, NEG))
    kw = kwin_ref[0, pl.ds(start, wlen), :]
    cend = lax.broadcasted_iota(jnp.int32, (1, nc), 1) * CMP_STRIDE + (CMP_BLOCK - 1)
    done = rep(jnp.where(cend <= t_col, 0.0, NEG))

    qs, o_c, o_w, imp_t, lhs = [], [], [], [], []
    for g in groups:
        in_group = (lane_q >= HEAD_DIM * g) & (lane_q < HEAD_DIM * (g + 1))
        parts = []
        for hl in range(HEADS_PER_GROUP):
            pair, side = divmod(hl, 2)
            c0 = g * HEADS_PER_GROUP * HEAD_DIM + pair * LANES
            qp = q[:, c0:c0 + LANES].astype(F32)
            if side != g:
                qp = pltpu.roll(qp, HEAD_DIM, axis=1)
            parts.append(jnp.where(in_group, qp, 0.0).astype(BF16))
        qs.append(jnp.concatenate(parts, axis=0))

        p = weights(_dot_nt(qs[g], kw) + band)
        o_w.append(normalized(_dot(p.astype(BF16), vwin_refs[g][0, pl.ds(start, wlen), :])))

        p = weights(_dot_nt(qs[g], kcmp_ref[0]) + done)
        both = _dot(p.astype(BF16), jnp.concatenate([vcmp_refs[g][0], ova_ref[...]], axis=1))
        o_c.append(normalized(both[:, 0:LANES]))

        share = normalized(both[:, LANES:2 * LANES])
        imp = share[0:Q_BLOCK]
        for hl in range(1, HEADS_PER_GROUP):
            imp = imp + share[hl * Q_BLOCK:(hl + 1) * Q_BLOCK]
        imp_t.append(imp.T[0:nb])

    def block_mask(imp_g, live):
        dead = [jnp.full((nb - live, Q_BLOCK), NEG, F32)] if live < nb else []
        dead.append(jnp.zeros((LANES - nb, Q_BLOCK), F32))
        if live <= n_sel:
            return jnp.concatenate([jnp.zeros((live, Q_BLOCK), F32)] + dead, axis=0)
        blk = lax.broadcasted_iota(jnp.int32, (live, Q_BLOCK), 0)
        cur = (t0 + lax.broadcasted_iota(jnp.int32, (live, Q_BLOCK), 1)) // SEL_BLOCK
        forced = (blk == 0) | (blk == cur) | (blk == cur - 1)
        imp_a = jnp.where(forced, FORCE, jnp.where(blk <= cur, imp_g[0:live], -FORCE))
        tiles = [imp_a[r * SUBLANES:(r + 1) * SUBLANES] for r in range(live // SUBLANES)]
        sub = lax.broadcasted_iota(jnp.int32, (SUBLANES, Q_BLOCK), 0)
        rank = [jnp.zeros((SUBLANES, Q_BLOCK), jnp.int32) for _ in tiles]
        for jp in range(live):
            other = jnp.broadcast_to(imp_a[jp:jp + 1, :], (SUBLANES, Q_BLOCK))
            for r, tile in enumerate(tiles):
                if r > jp // SUBLANES:
                    before = other >= tile
                elif r < jp // SUBLANES:
                    before = other > tile
                else:
                    before = (other > tile) | ((other == tile) & (sub > jp % SUBLANES))
                rank[r] = rank[r] + before.astype(jnp.int32)
        return jnp.concatenate([jnp.where(r < n_sel, 0.0, NEG) for r in rank] + dead, axis=0)

    live_steps = list(range(n_sel, nb + 1, n_sel))
    branches = [functools.partial(lambda live, *imps: tuple(block_mask(i, live) for i in imps), live)
                for live in live_steps]
    neg_t = lax.switch(((t0 + Q_BLOCK - 1) // SEL_BLOCK) // n_sel, branches, *imp_t)
    for g in groups:
        lhs.append(jnp.concatenate([qs[g], rep(neg_t[g].T.astype(BF16))], axis=1))

    def scores(kt):
        k0 = pl.multiple_of(kt * KEY_TILE, KEY_TILE)
        rhs = jnp.concatenate([ksel_ref[0, pl.ds(k0, KEY_TILE), :], hot_ref[pl.ds(k0, KEY_TILE), :]], axis=1)
        kpos = k0 + lax.broadcasted_iota(jnp.int32, (1, KEY_TILE), 1)
        causal = rep(jnp.where(kpos <= t_col, 0.0, NEG))
        return [_dot_nt(lhs[g], rhs) + causal for g in groups]

    def values(kt, g):
        return vsel_refs[g][0, pl.ds(pl.multiple_of(kt * KEY_TILE, KEY_TILE), KEY_TILE), :]

    n_tiles = (t0 + Q_BLOCK - 1) // KEY_TILE + 1
    zeros = tuple(jnp.zeros((rows, LANES), F32) for _ in groups)

    if bounded:
        def sweep(kt, acc):
            s_t = scores(kt)
            return tuple(acc[g] + _dot(jnp.exp2(s_t[g]).astype(BF16), values(kt, g)) for g in groups)

        acc_s = _loop_pairs(0, n_tiles, sweep, zeros)
    else:
        sbuf, = scratch

        lane_tiles = [slice(c * LANES, (c + 1) * LANES) for c in range(KEY_TILE // LANES)]

        def fold_max(m_run, s_t):
            return functools.reduce(jnp.maximum, [s_t[:, c] for c in lane_tiles], m_run)

        def pass1(kt, m_run):
            s_t = scores(kt)
            for g in groups:
                sbuf[g, kt] = s_t[g]
            return tuple(fold_max(m_run[g], s_t[g]) for g in groups)

        m_run = _loop_pairs(0, n_tiles, pass1, tuple(jnp.full((rows, LANES), NEG, F32) for _ in groups))
        m_sel = [jnp.broadcast_to(jnp.max(m_run[g], axis=1, keepdims=True), (rows, LANES)) for g in groups]

        def pass2(kt, acc):
            out = []
            for g in groups:
                s_g = sbuf[g, kt]
                p = jnp.concatenate([jnp.exp2(s_g[:, c] - m_sel[g]) for c in lane_tiles], axis=1).astype(BF16)
                out.append(acc[g] + _dot(p, values(kt, g)))
            return tuple(out)

        acc_s = _loop_pairs(0, n_tiles, pass2, zeros)

    for g in groups:
        o_s = normalized(acc_s[g])
        outs = []
        for hl in range(HEADS_PER_GROUP):
            h = g * HEADS_PER_GROUP + hl
            r = slice(hl * Q_BLOCK, (hl + 1) * Q_BLOCK)
            gcol = lambda br: gates[:, N_BRANCH * h + br:N_BRANCH * h + br + 1]
            outs.append(gcol(0) * o_c[g][r] + gcol(1) * o_s[r] + gcol(2) * o_w[g][r])
        for pair in range(HEADS_PER_GROUP // 2):
            left, right = outs[2 * pair], outs[2 * pair + 1]
            if g == 0:
                right = pltpu.roll(right, HEAD_DIM, axis=1)
            else:
                left = pltpu.roll(left, HEAD_DIM, axis=1)
            c0 = g * HEADS_PER_GROUP * HEAD_DIM + pair * LANES
            yb_ref[0, :, c0:c0 + LANES] = jnp.where(lane_q < HEAD_DIM, left, right).astype(BF16)


def _nsa_call(bounded, q, gates, kcmp, vcmp0, vcmp1, ksl, vsl0, vsl1, kwn, vwn0, vwn1, ova, hot):
    b, seq, _ = q.shape
    nc = seq // CMP_STRIDE
    wlen = min(WINDOW + Q_BLOCK, seq)
    n_sel = min(N_SEL, seq // SEL_BLOCK)
    qblk = lambda w: pl.BlockSpec((1, Q_BLOCK, w), lambda i, j: (i, j, 0))
    per_b = lambda rows: pl.BlockSpec((1, rows, KV_W), lambda i, j: (i, 0, 0))
    full = lambda a: pl.BlockSpec(a.shape, lambda i, j: (0,) * a.ndim)
    score_buf = pltpu.VMEM((N_GROUPS, seq // KEY_TILE, HEADS_PER_GROUP * Q_BLOCK, KEY_TILE), F32)
    return pl.pallas_call(
        functools.partial(_nsa_kernel, seq=seq, n_sel=n_sel, wlen=wlen, bounded=bounded),
        grid=(b, seq // Q_BLOCK),
        in_specs=[qblk(ATTN_W), qblk(GATE_PAD)] + [per_b(nc)] * 3 + [per_b(seq)] * 6 + [full(ova), full(hot)],
        out_specs=qblk(ATTN_W),
        out_shape=jax.ShapeDtypeStruct((b, seq, ATTN_W), BF16),
        scratch_shapes=[] if bounded else [score_buf],
        compiler_params=pltpu.CompilerParams(dimension_semantics=("arbitrary", "arbitrary"),
                                             vmem_limit_bytes=VMEM_LIMIT),
        name="nsa_bounded" if bounded else "nsa",
    )(q, gates, kcmp, vcmp0, vcmp1, ksl, vsl0, vsl1, kwn, vwn0, vwn1, ova, hot)


_FF_CHUNKS = ((0, 1024), (1024, 1024), (2048, 768))


def _mix_ffn_kernel(x_ref, ya_ref, yb_ref, gm_ref, wa_ref, wb_ref, wo_ref, g2_ref, win_ref, cw_ref, cb_ref, wout_ref,
                    out_ref, gbuf, *, tiles_per_seq, tm):
    i = pl.program_id(0)

    @pl.when(i % tiles_per_seq == 0)
    def _():
        gbuf[0:SUBLANES, :] = jnp.zeros((SUBLANES, D_FF), F32)

    gm = gm_ref[...].astype(F32)
    mix = gm[:, 0:D_MODEL] * _dot(ya_ref[...], wa_ref[...]) + gm[:, D_MODEL:] * _dot(yb_ref[...], wb_ref[...])
    x = x_ref[...] + _dot(mix.astype(BF16), wo_ref[...])

    ms = jnp.mean(x * x, axis=-1, keepdims=True)
    h = (x * lax.rsqrt(ms + EPS) * g2_ref[...]).astype(BF16)
    acc = jnp.zeros((tm, D_MODEL), F32)
    for c0, w in _FF_CHUNKS:
        cols = slice(c0, c0 + w)
        gpre = _dot(h, win_ref[:, c0:c0 + w])
        up = _dot(h, win_ref[:, D_FF + c0:D_FF + c0 + w])
        gbuf[SUBLANES:SUBLANES + tm, cols] = gpre
        cw = cw_ref[:, cols]
        conv = (cw[0:1, :] * gbuf[SUBLANES - 2:SUBLANES - 2 + tm, cols]
                + cw[1:2, :] * gbuf[SUBLANES - 1:SUBLANES - 1 + tm, cols]
                + cw[2:3, :] * gpre) + cb_ref[:, cols]
        gbuf[0:SUBLANES, cols] = gbuf[tm:tm + SUBLANES, cols]
        act = conv * jax.nn.sigmoid(conv) * up
        acc = acc + _dot(act.astype(BF16), wout_ref[c0:c0 + w, :])
    out_ref[...] = x + acc


def _mix_ffn_call(x2, ya, yb, gm, wa, wb, wo, g2, win, conv_w, conv_b, wout, seq):
    n = x2.shape[0]
    tm = min(ROW_TILE, seq)
    row = lambda w: pl.BlockSpec((tm, w), lambda i: (i, 0))
    full = lambda a: pl.BlockSpec(a.shape, lambda i: (0,) * a.ndim)
    once = lambda a: pl.BlockSpec(a.shape, lambda i: (0,) * a.ndim, pipeline_mode=pl.Buffered(1))
    return pl.pallas_call(
        functools.partial(_mix_ffn_kernel, tiles_per_seq=seq // tm, tm=tm),
        grid=(n // tm,),
        in_specs=[row(D_MODEL), row(CONV_W), row(ATTN_W), row(2 * D_MODEL), once(wa), once(wb), once(wo), full(g2),
                  once(win), full(conv_w), full(conv_b), once(wout)],
        out_specs=row(D_MODEL),
        out_shape=jax.ShapeDtypeStruct((n, D_MODEL), F32),
        scratch_shapes=[pltpu.VMEM((tm + SUBLANES, D_FF), F32)],
        compiler_params=pltpu.CompilerParams(dimension_semantics=("arbitrary",), vmem_limit_bytes=VMEM_LIMIT),
        name="mix_ffn",
    )(x2, ya, yb, gm, wa, wb, wo, g2, win, conv_w, conv_b, wout)


def _block_diag2(w):
    z = jnp.zeros_like(w)
    return jnp.concatenate([jnp.concatenate([w, z], axis=-1), jnp.concatenate([z, w], axis=-1)], axis=-2)


def _static_tables(seq):
    nc, nb = seq // CMP_STRIDE, seq // SEL_BLOCK
    n_cmp = (seq - CMP_BLOCK) // CMP_STRIDE + 1
    ii = np.arange(nc)[None, :] * CMP_STRIDE
    jj = np.arange(nb)[:, None] * SEL_BLOCK
    ovt = ((ii < jj + SEL_BLOCK) & (ii + CMP_BLOCK > jj) & (np.arange(nc)[None, :] < n_cmp)).astype(np.float32)
    ova = np.ones((nc, LANES), np.float32)
    ova[:, :HEAD_DIM] = 0.0
    ova[:, :nb] = ovt.T
    hot = (np.arange(LANES)[None, :] == (np.arange(seq) // SEL_BLOCK)[:, None]).astype(np.float32)
    head = np.arange(SEG_W) // HEAD_DIM
    seg = (head[:, None] == head[None, :]).astype(np.float32)
    return jnp.asarray(ova, BF16), jnp.asarray(hot, BF16), jnp.asarray(seg, BF16)


def kernel(x, positions, norm1_g, w_in, conv_a_w, w_a_out, q_norm_g, k_norm_g, cmp_k_pe, cmp_k_w1, cmp_k_w2, cmp_v_pe, cmp_v_w1, cmp_v_w2, w_b_out, w_o, norm2_g, w_ffn_in, ffn_conv_w, ffn_conv_b, w_ffn_out):
    b, seq, d = x.shape
    n = b * seq
    half = HEAD_DIM // 2
    inv = ROPE_THETA ** (-jnp.arange(half, dtype=F32) / half)
    inv_lanes = jnp.tile(inv, LANES // half)[None, :]
    posf = positions.astype(F32).reshape(n, 1)
    ova, hot, seg = _static_tables(seq)
    gate_cols = N_BRANCH * N_HEADS

    x2 = x.reshape(n, d)
    for l in range(w_in.shape[0]):
        w = w_in[l]
        w_main = w[:, :_GATE_OFF].astype(BF16)
        w_gate = jnp.pad(w[:, _GATE_OFF:_GATE_OFF + gate_cols], ((0, 0), (0, GATE_PAD - gate_cols))).astype(BF16)
        w_gm = w[:, _GATE_OFF + gate_cols:].astype(BF16)
        qg = jnp.tile(q_norm_g[l], N_HEADS)[None, :]
        kg = jnp.tile(k_norm_g[l], (1, N_GROUPS))
        ya, q, kc, vc, ksl, vsl0, vsl1, kwn, vwn0, vwn1, gates, gm = _proj_call(
            x2, posf, norm1_g[l][None, :], w_main, w_gate, w_gm, conv_a_w[l], qg, kg[1:3].reshape(1, 2 * KV_W),
            inv_lanes, seg, seq)

        per_b = lambda a: a.reshape(b, seq, a.shape[-1])
        w1_blocks = lambda w1: _block_diag2(w1.reshape(CMP_BLOCK, HEAD_DIM, HEAD_DIM)).astype(BF16)
        kcmp, vcmp0, vcmp1 = _compress_call(
            per_b(kc), per_b(vc), jnp.tile(cmp_k_pe[l], (1, N_GROUPS)), jnp.tile(cmp_v_pe[l], (1, N_GROUPS)),
            w1_blocks(cmp_k_w1[l]), _block_diag2(cmp_k_w2[l]).astype(BF16),
            w1_blocks(cmp_v_w1[l]), _block_diag2(cmp_v_w2[l]).astype(BF16), kg[0:1], seg)

        reach = HEAD_DIM * (HEAD_DIM ** -0.5 * LOG2E) * jnp.max(jnp.abs(q_norm_g[l])) * jnp.max(jnp.abs(k_norm_g[l]))
        nsa_args = (per_b(q), per_b(gates), kcmp, vcmp0, vcmp1, per_b(ksl), per_b(vsl0), per_b(vsl1), per_b(kwn),
                    per_b(vwn0), per_b(vwn1), ova, hot)
        yb = lax.cond(reach * 1.02 <= SCORE_BOUND, functools.partial(_nsa_call, True),
                      functools.partial(_nsa_call, False), *nsa_args)

        x2 = _mix_ffn_call(x2, ya, yb.reshape(n, ATTN_W), gm, w_a_out[l].astype(BF16), w_b_out[l].astype(BF16),
                           w_o[l].astype(BF16), norm2_g[l][None, :], w_ffn_in[l].astype(BF16), ffn_conv_w[l],
                           ffn_conv_b[l][None, :], w_ffn_out[l].astype(BF16), seq)
    return x2.reshape(b, seq, d)
```

```python
import functools

import jax
import jax.numpy as jnp
import numpy as np
from jax import lax
from jax.experimental import pallas as pl
from jax.experimental.pallas import tpu as pltpu

F32 = jnp.float32
BF16 = jnp.bfloat16

D_MODEL = 1024
CONV_W = 512
CONV_K = 3
N_HEADS = 8
HEAD_DIM = 64
N_GROUPS = 2
HEADS_PER_GROUP = N_HEADS // N_GROUPS
ATTN_W = N_HEADS * HEAD_DIM
KV_W = N_GROUPS * HEAD_DIM
CMP_BLOCK = 32
CMP_STRIDE = 16
SEL_BLOCK = 64
N_SEL = 16
WINDOW = 512
Q_BLOCK = 128
N_BRANCH = 3
ROPE_THETA = 10000.0
D_FF = 2816
EPS = 1e-6
NEG = -1e30
FORCE = 1e9
LOG2E = 1.4426950408889634
SCORE_BOUND = 60.0

LANES = 128
SUBLANES = 8
KEY_TILE = 512
SEG_W = 256
ROW_TILE = 512
GATE_PAD = LANES
VMEM_LIMIT = 56 * 1024 * 1024

_A_END = 3 * CONV_W
_Q_OFF = _A_END
_KV_OFF = _Q_OFF + ATTN_W
_GATE_OFF = _KV_OFF + 6 * KV_W


def _dot(a, b):
    return jnp.dot(a, b, preferred_element_type=F32)


def _dot_nt(a, b):
    return lax.dot_general(a, b, (((1,), (1,)), ((), ())), preferred_element_type=F32)


def _head_norm(x, gain, seg):
    sq = x * x
    hi = sq.astype(BF16)
    lo = (sq - hi.astype(F32)).astype(BF16)
    ms = (_dot(hi, seg) + _dot(lo, seg)) * (1.0 / HEAD_DIM)
    return x * lax.rsqrt(ms + EPS) * gain


def _rope(x, cos, sin_signed):
    lane = lax.broadcasted_iota(jnp.int32, x.shape, 1)
    half = HEAD_DIM // 2
    rot = jnp.where((lane & half) != 0, pltpu.roll(x, half, axis=1), pltpu.roll(x, LANES - half, axis=1))
    return x * cos + rot * sin_signed


def _proj_kernel(x_ref, pos_ref, g1_ref, w_ref, wgate_ref, wgm_ref, cw_ref, qg_ref, kg_ref, inv_ref, seg_ref,
                 ya_ref, q_ref, kc_ref, vc_ref, ksl_ref, vsl0_ref, vsl1_ref, kwn_ref, vwn0_ref, vwn1_ref, gate_ref,
                 gm_ref, ubuf, *, tiles_per_seq, tm):
    i = pl.program_id(0)

    @pl.when(i % tiles_per_seq == 0)
    def _():
        ubuf[0:SUBLANES, :] = jnp.zeros((SUBLANES, CONV_W), F32)

    x = x_ref[...]
    ms = jnp.mean(x * x, axis=-1, keepdims=True)
    h = (x * lax.rsqrt(ms + EPS) * g1_ref[...]).astype(BF16)

    pa = _dot(h, w_ref[:, 0:_A_END])
    u = pa[:, CONV_W:2 * CONV_W] * pa[:, 2 * CONV_W:3 * CONV_W]
    ubuf[SUBLANES:SUBLANES + tm, :] = u
    cw = cw_ref[...]
    y = (cw[0:1, :] * ubuf[SUBLANES - 2:SUBLANES - 2 + tm, :]
         + cw[1:2, :] * ubuf[SUBLANES - 1:SUBLANES - 1 + tm, :]
         + cw[2:3, :] * u)
    ya_ref[...] = (pa[:, 0:CONV_W] * y).astype(BF16)
    ubuf[0:SUBLANES, :] = ubuf[tm:tm + SUBLANES, :]

    ang = pos_ref[...] * inv_ref[...]
    lane = lax.broadcasted_iota(jnp.int32, ang.shape, 1)
    cos = jnp.cos(ang)
    sin_signed = jnp.where((lane & (HEAD_DIM // 2)) != 0, jnp.sin(ang), -jnp.sin(ang))

    gm_ref[...] = jax.nn.sigmoid(_dot(h, wgm_ref[...])).astype(BF16)
    pb = _dot(h, w_ref[:, _Q_OFF:_GATE_OFF])
    scale = HEAD_DIM ** -0.5 * LOG2E
    seg = seg_ref[...]
    for c in range(ATTN_W // SEG_W):
        cols = slice(c * SEG_W, (c + 1) * SEG_W)
        qn = _head_norm(pb[:, cols], qg_ref[:, cols], seg)
        for t in range(SEG_W // LANES):
            lanes = slice(t * LANES, (t + 1) * LANES)
            q_ref[:, c * SEG_W + t * LANES:c * SEG_W + (t + 1) * LANES] = (
                _rope(qn[:, lanes], cos, sin_signed) * scale).astype(BF16)
    kv = pb[:, ATTN_W:ATTN_W + 6 * KV_W]
    kc_ref[...] = kv[:, 0:KV_W]
    vc_ref[...] = kv[:, KV_W:2 * KV_W]
    kn = _head_norm(jnp.concatenate([kv[:, 2 * KV_W:3 * KV_W], kv[:, 4 * KV_W:5 * KV_W]], axis=1), kg_ref[...], seg)
    ksl_ref[...] = _rope(kn[:, 0:KV_W], cos, sin_signed).astype(BF16)
    kwn_ref[...] = _rope(kn[:, KV_W:2 * KV_W], cos, sin_signed).astype(BF16)
    first = lane < HEAD_DIM
    for v, out0, out1 in ((kv[:, 3 * KV_W:4 * KV_W], vsl0_ref, vsl1_ref), (kv[:, 5 * KV_W:6 * KV_W], vwn0_ref, vwn1_ref)):
        out0[...] = jnp.where(first, v, 1.0).astype(BF16)
        out1[...] = jnp.where(first, 1.0, v).astype(BF16)
    gate_ref[...] = jax.nn.sigmoid(_dot(h, wgate_ref[...]))


def _proj_call(x2, posf, g1, w_main, w_gate, w_gm, conv_w, qg, kg, inv, seg, seq):
    n = x2.shape[0]
    tm = min(ROW_TILE, seq)
    row = lambda w: pl.BlockSpec((tm, w), lambda i: (i, 0))
    full = lambda a: pl.BlockSpec(a.shape, lambda i: (0,) * a.ndim)
    once = lambda a: pl.BlockSpec(a.shape, lambda i: (0,) * a.ndim, pipeline_mode=pl.Buffered(1))
    out_shapes = (
        jax.ShapeDtypeStruct((n, CONV_W), BF16),
        jax.ShapeDtypeStruct((n, ATTN_W), BF16),
        jax.ShapeDtypeStruct((n, KV_W), F32),
        jax.ShapeDtypeStruct((n, KV_W), F32),
        jax.ShapeDtypeStruct((n, KV_W), BF16),
        jax.ShapeDtypeStruct((n, KV_W), BF16),
        jax.ShapeDtypeStruct((n, KV_W), BF16),
        jax.ShapeDtypeStruct((n, KV_W), BF16),
        jax.ShapeDtypeStruct((n, KV_W), BF16),
        jax.ShapeDtypeStruct((n, KV_W), BF16),
        jax.ShapeDtypeStruct((n, GATE_PAD), F32),
        jax.ShapeDtypeStruct((n, 2 * D_MODEL), BF16),
    )
    return pl.pallas_call(
        functools.partial(_proj_kernel, tiles_per_seq=seq // tm, tm=tm),
        grid=(n // tm,),
        in_specs=[row(D_MODEL), row(1), full(g1), once(w_main), once(w_gate), once(w_gm), full(conv_w), full(qg),
                  full(kg), full(inv), full(seg)],
        out_specs=[row(s.shape[1]) for s in out_shapes],
        out_shape=out_shapes,
        scratch_shapes=[pltpu.VMEM((tm + SUBLANES, CONV_W), F32)],
        compiler_params=pltpu.CompilerParams(dimension_semantics=("arbitrary",), vmem_limit_bytes=VMEM_LIMIT),
        name="proj",
    )(x2, posf, g1, w_main, w_gate, w_gm, conv_w, qg, kg, inv, seg)


def _compress_kernel(kc_ref, vc_ref, pek_ref, pev_ref, w1k_ref, w2k_ref, w1v_ref, w2v_ref, kg_ref, seg_ref,
                     kcmp_ref, vcmp0_ref, vcmp1_ref, buf, *, seq):
    nc = seq // CMP_STRIDE
    buf[seq:seq + CMP_BLOCK, :] = jnp.zeros((CMP_BLOCK, KV_W), F32)
    for src, pe, w1, w2, is_key in ((kc_ref, pek_ref, w1k_ref, w2k_ref, True), (vc_ref, pev_ref, w1v_ref, w2v_ref, False)):
        buf[0:seq, :] = src[0]
        acc = jnp.zeros((nc, KV_W), F32)
        for r in range(CMP_BLOCK):
            xr = buf[pl.ds(r, nc, stride=CMP_STRIDE), :] + pe[r:r + 1, :]
            acc = acc + _dot(xr.astype(BF16), w1[r])
        hmid = acc * jax.nn.sigmoid(acc)
        o = _dot(hmid.astype(BF16), w2[...])
        if is_key:
            kcmp_ref[0] = _head_norm(o, kg_ref[...], seg_ref[0:KV_W, 0:KV_W]).astype(BF16)
        else:
            first = lax.broadcasted_iota(jnp.int32, o.shape, 1) < HEAD_DIM
            vcmp0_ref[0] = jnp.where(first, o, 1.0).astype(BF16)
            vcmp1_ref[0] = jnp.where(first, 1.0, o).astype(BF16)


def _compress_call(kc, vc, pek, pev, w1k, w2k, w1v, w2v, kg0, seg):
    b, seq, _ = kc.shape
    nc = seq // CMP_STRIDE
    per_b = lambda rows: pl.BlockSpec((1, rows, KV_W), lambda i: (i, 0, 0))
    full = lambda a: pl.BlockSpec(a.shape, lambda i: (0,) * a.ndim)
    return pl.pallas_call(
        functools.partial(_compress_kernel, seq=seq),
        grid=(b,),
        in_specs=[per_b(seq), per_b(seq), full(pek), full(pev), full(w1k), full(w2k), full(w1v), full(w2v), full(kg0),
                  full(seg)],
        out_specs=[per_b(nc)] * 3,
        out_shape=(jax.ShapeDtypeStruct((b, nc, KV_W), BF16),) * 3,
        scratch_shapes=[pltpu.VMEM((seq + CMP_BLOCK, KV_W), F32)],
        compiler_params=pltpu.CompilerParams(dimension_semantics=("arbitrary",), vmem_limit_bytes=VMEM_LIMIT),
        name="compress",
    )(kc, vc, pek, pev, w1k, w2k, w1v, w2v, kg0, seg)


def _loop_pairs(lo, hi, step, carry):
    pairs = (hi - lo) // 2

    def two(i, c):
        return step(lo + 2 * i + 1, step(lo + 2 * i, c))

    return lax.fori_loop(lo + 2 * pairs, hi, step, lax.fori_loop(0, pairs, two, carry))


def _nsa_kernel(q_ref, gate_ref, kcmp_ref, vcmp0_ref, vcmp1_ref, ksel_ref, vsel0_ref, vsel1_ref, kwin_ref, vwin0_ref,
                vwin1_ref, ova_ref, hot_ref, yb_ref, *scratch, seq, n_sel, wlen, bounded):
    qb = pl.program_id(1)
    t0 = qb * Q_BLOCK
    nc = seq // CMP_STRIDE
    nb = seq // SEL_BLOCK
    rows = HEADS_PER_GROUP * Q_BLOCK
    groups = range(N_GROUPS)
    vcmp_refs = (vcmp0_ref, vcmp1_ref)
    vsel_refs = (vsel0_ref, vsel1_ref)
    vwin_refs = (vwin0_ref, vwin1_ref)

    q = q_ref[0]
    gates = gate_ref[0]
    lane_q = lax.broadcasted_iota(jnp.int32, (Q_BLOCK, LANES), 1)
    t_col = t0 + lax.broadcasted_iota(jnp.int32, (Q_BLOCK, 1), 0)
    rep = lambda a: jnp.concatenate([a] * HEADS_PER_GROUP, axis=0)

    def weights(s):
        if bounded:
            return jnp.exp2(s)
        m = jnp.maximum(jnp.max(s, axis=1, keepdims=True), 0.5 * NEG)
        return jnp.exp2(s - m)

    def normalized(acc):
        l = pltpu.roll(acc, HEAD_DIM, axis=1)
        return acc * jnp.where(l > 0.0, 1.0 / l, 0.0)

    def gated(o, g, branch):
        cols = [N_BRANCH * (g * HEADS_PER_GROUP + hl) + branch for hl in range(HEADS_PER_GROUP)]
        return jnp.concatenate([gates[:, c:c + 1] * o[hl * Q_BLOCK:(hl + 1) * Q_BLOCK] for hl, c in enumerate(cols)],
                               axis=0)

    cend = lax.broadcasted_iota(jnp.int32, (1, nc), 1) * CMP_STRIDE + (CMP_BLOCK - 1)
    done = rep(jnp.where(cend <= t_col, 0.0, NEG))

    qs, o_c, o_w, imp_t, lhs = [], [], [], [], []
    for g in groups:
        in_group = (lane_q >= HEAD_DIM * g) & (lane_q < HEAD_DIM * (g + 1))
        parts = []
        for hl in range(HEADS_PER_GROUP):
            pair, side = divmod(hl, 2)
            c0 = g * HEADS_PER_GROUP * HEAD_DIM + pair * LANES
            qp = q[:, c0:c0 + LANES].astype(F32)
            if side != g:
                qp = pltpu.roll(qp, HEAD_DIM, axis=1)
            parts.append(jnp.where(in_group, qp, 0.0).astype(BF16))
        qs.append(jnp.concatenate(parts, axis=0))

        p = weights(_dot_nt(qs[g], kcmp_ref[0]) + done)
        both = _dot(p.astype(BF16), jnp.concatenate([vcmp_refs[g][0], ova_ref[...]], axis=1))
        o_c.append(gated(normalized(both[:, 0:LANES]), g, 0))

        share = normalized(both[:, LANES:2 * LANES])
        imp = share[0:Q_BLOCK]
        for hl in range(1, HEADS_PER_GROUP):
            imp = imp + share[hl * Q_BLOCK:(hl + 1) * Q_BLOCK]
        imp_t.append(imp.T[0:nb])

    def block_mask(imp_g, live):
        dead = [jnp.full((nb - live, Q_BLOCK), NEG, F32)] if live < nb else []
        dead.append(jnp.zeros((LANES - nb, Q_BLOCK), F32))
        if live <= n_sel:
            return jnp.concatenate([jnp.zeros((live, Q_BLOCK), F32)] + dead, axis=0)
        blk = lax.broadcasted_iota(jnp.int32, (live, Q_BLOCK), 0)
        cur = (t0 + lax.broadcasted_iota(jnp.int32, (live, Q_BLOCK), 1)) // SEL_BLOCK
        forced = (blk == 0) | (blk == cur) | (blk == cur - 1)
        imp_a = jnp.where(forced, FORCE, jnp.where(blk <= cur, imp_g[0:live], -FORCE))
        tiles = [imp_a[r * SUBLANES:(r + 1) * SUBLANES] for r in range(live // SUBLANES)]
        sub = lax.broadcasted_iota(jnp.int32, (SUBLANES, Q_BLOCK), 0)
        rank = [jnp.zeros((SUBLANES, Q_BLOCK), jnp.int32) for _ in tiles]
        for jp in range(live):
            other = jnp.broadcast_to(imp_a[jp:jp + 1, :], (SUBLANES, Q_BLOCK))
            for r, tile in enumerate(tiles):
                if r > jp // SUBLANES:
                    before = other >= tile
                elif r < jp // SUBLANES:
                    before = other > tile
                else:
                    before = (other > tile) | ((other == tile) & (sub > jp % SUBLANES))
                rank[r] = rank[r] + before.astype(jnp.int32)
        return jnp.concatenate([jnp.where(r < n_sel, 0.0, NEG) for r in rank] + dead, axis=0)

    live_steps = list(range(n_sel, nb + 1, n_sel))
    branches = [functools.partial(lambda live, *imps: tuple(block_mask(i, live) for i in imps), live)
                for live in live_steps]
    neg_t = lax.switch(((t0 + Q_BLOCK - 1) // SEL_BLOCK) // n_sel, branches, *imp_t)
    for g in groups:
        lhs.append(jnp.concatenate([qs[g], rep(neg_t[g].T.astype(BF16))], axis=1))

    def scores(kt):
        k0 = pl.multiple_of(kt * KEY_TILE, KEY_TILE)
        rhs = jnp.concatenate([ksel_ref[0, pl.ds(k0, KEY_TILE), :], hot_ref[pl.ds(k0, KEY_TILE), :]], axis=1)
        kpos = k0 + lax.broadcasted_iota(jnp.int32, (1, KEY_TILE), 1)
        causal = rep(jnp.where(kpos <= t_col, 0.0, NEG))
        return [_dot_nt(lhs[g], rhs) + causal for g in groups]

    def values(kt, g):
        return vsel_refs[g][0, pl.ds(pl.multiple_of(kt * KEY_TILE, KEY_TILE), KEY_TILE), :]

    n_tiles = (t0 + Q_BLOCK - 1) // KEY_TILE + 1
    zeros = tuple(jnp.zeros((rows, LANES), F32) for _ in groups)

    if bounded:
        def sweep(kt, acc):
            s_t = scores(kt)
            return tuple(acc[g] + _dot(jnp.exp2(s_t[g]).astype(BF16), values(kt, g)) for g in groups)

        acc_s = _loop_pairs(0, n_tiles, sweep, zeros)
    else:
        sbuf, = scratch

        lane_tiles = [slice(c * LANES, (c + 1) * LANES) for c in range(KEY_TILE // LANES)]

        def fold_max(m_run, s_t):
            return functools.reduce(jnp.maximum, [s_t[:, c] for c in lane_tiles], m_run)

        def pass1(kt, m_run):
            s_t = scores(kt)
            for g in groups:
                sbuf[g, kt] = s_t[g]
            return tuple(fold_max(m_run[g], s_t[g]) for g in groups)

        m_run = _loop_pairs(0, n_tiles, pass1, tuple(jnp.full((rows, LANES), NEG, F32) for _ in groups))
        m_sel = [jnp.broadcast_to(jnp.max(m_run[g], axis=1, keepdims=True), (rows, LANES)) for g in groups]

        def pass2(kt, acc):
            out = []
            for g in groups:
                s_g = sbuf[g, kt]
                p = jnp.concatenate([jnp.exp2(s_g[:, c] - m_sel[g]) for c in lane_tiles], axis=1).astype(BF16)
                out.append(acc[g] + _dot(p, values(kt, g)))
            return tuple(out)

        acc_s = _loop_pairs(0, n_tiles, pass2, zeros)

    start = pl.multiple_of(jnp.maximum(t0 + Q_BLOCK - wlen, 0), Q_BLOCK)
    diff = t_col - (start + lax.broadcasted_iota(jnp.int32, (1, wlen), 1))
    band = rep(jnp.where((diff >= 0) & (diff < WINDOW), 0.0, NEG))
    kw = kwin_ref[0, pl.ds(start, wlen), :]
    for g in groups:
        p = weights(_dot_nt(qs[g], kw) + band)
        o_w.append(gated(normalized(_dot(p.astype(BF16), vwin_refs[g][0, pl.ds(start, wlen), :])), g, 2))

    for g in groups:
        out = o_c[g] + gated(normalized(acc_s[g]), g, 1) + o_w[g]
        for pair in range(HEADS_PER_GROUP // 2):
            left = out[2 * pair * Q_BLOCK:(2 * pair + 1) * Q_BLOCK]
            right = out[(2 * pair + 1) * Q_BLOCK:(2 * pair + 2) * Q_BLOCK]
            if g == 0:
                right = pltpu.roll(right, HEAD_DIM, axis=1)
            else:
                left = pltpu.roll(left, HEAD_DIM, axis=1)
            c0 = g * HEADS_PER_GROUP * HEAD_DIM + pair * LANES
            yb_ref[0, :, c0:c0 + LANES] = jnp.where(lane_q < HEAD_DIM, left, right).astype(BF16)


def _nsa_call(bounded, q, gates, kcmp, vcmp0, vcmp1, ksl, vsl0, vsl1, kwn, vwn0, vwn1, ova, hot):
    b, seq, _ = q.shape
    nc = seq // CMP_STRIDE
    wlen = min(WINDOW + Q_BLOCK, seq)
    n_sel = min(N_SEL, seq // SEL_BLOCK)
    qblk = lambda w: pl.BlockSpec((1, Q_BLOCK, w), lambda i, j: (i, j, 0))
    per_b = lambda rows: pl.BlockSpec((1, rows, KV_W), lambda i, j: (i, 0, 0))
    full = lambda a: pl.BlockSpec(a.shape, lambda i, j: (0,) * a.ndim)
    score_buf = pltpu.VMEM((N_GROUPS, seq // KEY_TILE, HEADS_PER_GROUP * Q_BLOCK, KEY_TILE), F32)
    return pl.pallas_call(
        functools.partial(_nsa_kernel, seq=seq, n_sel=n_sel, wlen=wlen, bounded=bounded),
        grid=(b, seq // Q_BLOCK),
        in_specs=[qblk(ATTN_W), qblk(GATE_PAD)] + [per_b(nc)] * 3 + [per_b(seq)] * 6 + [full(ova), full(hot)],
        out_specs=qblk(ATTN_W),
        out_shape=jax.ShapeDtypeStruct((b, seq, ATTN_W), BF16),
        scratch_shapes=[] if bounded else [score_buf],
        compiler_params=pltpu.CompilerParams(dimension_semantics=("arbitrary", "arbitrary"),
                                             vmem_limit_bytes=VMEM_LIMIT),
        name="nsa_bounded" if bounded else "nsa",
    )(q, gates, kcmp, vcmp0, vcmp1, ksl, vsl0, vsl1, kwn, vwn0, vwn1, ova, hot)


_FF_CHUNKS = ((0, 1024), (1024, 1024), (2048, 768))


def _mix_ffn_kernel(x_ref, ya_ref, yb_ref, gm_ref, wa_ref, wb_ref, wo_ref, g2_ref, win_ref, cw_ref, cb_ref, wout_ref,
                    out_ref, gbuf, *, tiles_per_seq, tm):
    i = pl.program_id(0)

    @pl.when(i % tiles_per_seq == 0)
    def _():
        gbuf[0:SUBLANES, :] = jnp.zeros((SUBLANES, D_FF), F32)

    gm = gm_ref[...].astype(F32)
    mix = gm[:, 0:D_MODEL] * _dot(ya_ref[...], wa_ref[...]) + gm[:, D_MODEL:] * _dot(yb_ref[...], wb_ref[...])
    x = x_ref[...] + _dot(mix.astype(BF16), wo_ref[...])

    ms = jnp.mean(x * x, axis=-1, keepdims=True)
    h = (x * lax.rsqrt(ms + EPS) * g2_ref[...]).astype(BF16)
    acc = jnp.zeros((tm, D_MODEL), F32)
    for c0, w in _FF_CHUNKS:
        cols = slice(c0, c0 + w)
        gpre = _dot(h, win_ref[:, c0:c0 + w])
        up = _dot(h, win_ref[:, D_FF + c0:D_FF + c0 + w])
        gbuf[SUBLANES:SUBLANES + tm, cols] = gpre
        cw = cw_ref[:, cols]
        conv = (cw[0:1, :] * gbuf[SUBLANES - 2:SUBLANES - 2 + tm, cols]
                + cw[1:2, :] * gbuf[SUBLANES - 1:SUBLANES - 1 + tm, cols]
                + cw[2:3, :] * gpre) + cb_ref[:, cols]
        gbuf[0:SUBLANES, cols] = gbuf[tm:tm + SUBLANES, cols]
        act = conv * jax.nn.sigmoid(conv) * up
        acc = acc + _dot(act.astype(BF16), wout_ref[c0:c0 + w, :])
    out_ref[...] = x + acc


def _mix_ffn_call(x2, ya, yb, gm, wa, wb, wo, g2, win, conv_w, conv_b, wout, seq):
    n = x2.shape[0]
    tm = min(ROW_TILE, seq)
    row = lambda w: pl.BlockSpec((tm, w), lambda i: (i, 0))
    full = lambda a: pl.BlockSpec(a.shape, lambda i: (0,) * a.ndim)
    once = lambda a: pl.BlockSpec(a.shape, lambda i: (0,) * a.ndim, pipeline_mode=pl.Buffered(1))
    return pl.pallas_call(
        functools.partial(_mix_ffn_kernel, tiles_per_seq=seq // tm, tm=tm),
        grid=(n // tm,),
        in_specs=[row(D_MODEL), row(CONV_W), row(ATTN_W), row(2 * D_MODEL), once(wa), once(wb), once(wo), full(g2),
                  once(win), full(conv_w), full(conv_b), once(wout)],
        out_specs=row(D_MODEL),
        out_shape=jax.ShapeDtypeStruct((n, D_MODEL), F32),
        scratch_shapes=[pltpu.VMEM((tm + SUBLANES, D_FF), F32)],
        compiler_params=pltpu.CompilerParams(dimension_semantics=("arbitrary",), vmem_limit_bytes=VMEM_LIMIT),
        name="mix_ffn",
    )(x2, ya, yb, gm, wa, wb, wo, g2, win, conv_w, conv_b, wout)


def _block_diag2(w):
    z = jnp.zeros_like(w)
    return jnp.concatenate([jnp.concatenate([w, z], axis=-1), jnp.concatenate([z, w], axis=-1)], axis=-2)


def _static_tables(seq):
    nc, nb = seq // CMP_STRIDE, seq // SEL_BLOCK
    n_cmp = (seq - CMP_BLOCK) // CMP_STRIDE + 1
    ii = np.arange(nc)[None, :] * CMP_STRIDE
    jj = np.arange(nb)[:, None] * SEL_BLOCK
    ovt = ((ii < jj + SEL_BLOCK) & (ii + CMP_BLOCK > jj) & (np.arange(nc)[None, :] < n_cmp)).astype(np.float32)
    ova = np.ones((nc, LANES), np.float32)
    ova[:, :HEAD_DIM] = 0.0
    ova[:, :nb] = ovt.T
    hot = (np.arange(LANES)[None, :] == (np.arange(seq) // SEL_BLOCK)[:, None]).astype(np.float32)
    head = np.arange(SEG_W) // HEAD_DIM
    seg = (head[:, None] == head[None, :]).astype(np.float32)
    return jnp.asarray(ova, BF16), jnp.asarray(hot, BF16), jnp.asarray(seg, BF16)


def kernel(x, positions, norm1_g, w_in, conv_a_w, w_a_out, q_norm_g, k_norm_g, cmp_k_pe, cmp_k_w1, cmp_k_w2, cmp_v_pe, cmp_v_w1, cmp_v_w2, w_b_out, w_o, norm2_g, w_ffn_in, ffn_conv_w, ffn_conv_b, w_ffn_out):
    b, seq, d = x.shape
    n = b * seq
    half = HEAD_DIM // 2
    inv = ROPE_THETA ** (-jnp.arange(half, dtype=F32) / half)
    inv_lanes = jnp.tile(inv, LANES // half)[None, :]
    posf = positions.astype(F32).reshape(n, 1)
    ova, hot, seg = _static_tables(seq)
    gate_cols = N_BRANCH * N_HEADS

    x2 = x.reshape(n, d)
    for l in range(w_in.shape[0]):
        w = w_in[l]
        w_main = w[:, :_GATE_OFF].astype(BF16)
        w_gate = jnp.pad(w[:, _GATE_OFF:_GATE_OFF + gate_cols], ((0, 0), (0, GATE_PAD - gate_cols))).astype(BF16)
        w_gm = w[:, _GATE_OFF + gate_cols:].astype(BF16)
        qg = jnp.tile(q_norm_g[l], N_HEADS)[None, :]
        kg = jnp.tile(k_norm_g[l], (1, N_GROUPS))
        ya, q, kc, vc, ksl, vsl0, vsl1, kwn, vwn0, vwn1, gates, gm = _proj_call(
            x2, posf, norm1_g[l][None, :], w_main, w_gate, w_gm, conv_a_w[l], qg, kg[1:3].reshape(1, 2 * KV_W),
            inv_lanes, seg, seq)

        per_b = lambda a: a.reshape(b, seq, a.shape[-1])
        w1_blocks = lambda w1: _block_diag2(w1.reshape(CMP_BLOCK, HEAD_DIM, HEAD_DIM)).astype(BF16)
        kcmp, vcmp0, vcmp1 = _compress_call(
            per_b(kc), per_b(vc), jnp.tile(cmp_k_pe[l], (1, N_GROUPS)), jnp.tile(cmp_v_pe[l], (1, N_GROUPS)),
            w1_blocks(cmp_k_w1[l]), _block_diag2(cmp_k_w2[l]).astype(BF16),
            w1_blocks(cmp_v_w1[l]), _block_diag2(cmp_v_w2[l]).astype(BF16), kg[0:1], seg)

        reach = HEAD_DIM * (HEAD_DIM ** -0.5 * LOG2E) * jnp.max(jnp.abs(q_norm_g[l])) * jnp.max(jnp.abs(k_norm_g[l]))
        nsa_args = (per_b(q), per_b(gates), kcmp, vcmp0, vcmp1, per_b(ksl), per_b(vsl0), per_b(vsl1), per_b(kwn),
                    per_b(vwn0), per_b(vwn1), ova, hot)
        yb = lax.cond(reach * 1.02 <= SCORE_BOUND, functools.partial(_nsa_call, True),
                      functools.partial(_nsa_call, False), *nsa_args)

        x2 = _mix_ffn_call(x2, ya, yb.reshape(n, ATTN_W), gm, w_a_out[l].astype(BF16), w_b_out[l].astype(BF16),
                           w_o[l].astype(BF16), norm2_g[l][None, :], w_ffn_in[l].astype(BF16), ffn_conv_w[l],
                           ffn_conv_b[l][None, :], w_ffn_out[l].astype(BF16), seq)
    return x2.reshape(b, seq, d)
```

```python
import functools

import jax
import jax.numpy as jnp
import numpy as np
from jax import lax
from jax.experimental import pallas as pl
from jax.experimental.pallas import tpu as pltpu

F32 = jnp.float32
BF16 = jnp.bfloat16

D_MODEL = 1024
CONV_W = 512
CONV_K = 3
N_HEADS = 8
HEAD_DIM = 64
N_GROUPS = 2
HEADS_PER_GROUP = N_HEADS // N_GROUPS
ATTN_W = N_HEADS * HEAD_DIM
KV_W = N_GROUPS * HEAD_DIM
CMP_BLOCK = 32
CMP_STRIDE = 16
SEL_BLOCK = 64
N_SEL = 16
WINDOW = 512
Q_BLOCK = 128
SWEEP_Q_BLOCK = 256
N_BRANCH = 3
ROPE_THETA = 10000.0
D_FF = 2816
EPS = 1e-6
NEG = -1e30
FORCE = 1e9
LOG2E = 1.4426950408889634
SCORE_BOUND = 60.0

LANES = 128
SUBLANES = 8
KEY_TILE = 512
SEG_W = 256
ROW_TILE = 512
GATE_PAD = LANES
VMEM_LIMIT = 56 * 1024 * 1024

_A_END = 3 * CONV_W
_Q_OFF = _A_END
_KV_OFF = _Q_OFF + ATTN_W
_GATE_OFF = _KV_OFF + 6 * KV_W


def _dot(a, b):
    return jnp.dot(a, b, preferred_element_type=F32)


def _dot_nt(a, b):
    return lax.dot_general(a, b, (((1,), (1,)), ((), ())), preferred_element_type=F32)


def _head_norm(x, gain, seg):
    sq = x * x
    hi = sq.astype(BF16)
    lo = (sq - hi.astype(F32)).astype(BF16)
    ms = (_dot(hi, seg) + _dot(lo, seg)) * (1.0 / HEAD_DIM)
    return x * lax.rsqrt(ms + EPS) * gain


def _rope(x, cos, sin_signed):
    lane = lax.broadcasted_iota(jnp.int32, x.shape, 1)
    half = HEAD_DIM // 2
    rot = jnp.where((lane & half) != 0, pltpu.roll(x, half, axis=1), pltpu.roll(x, LANES - half, axis=1))
    return x * cos + rot * sin_signed


def _proj_kernel(x_ref, pos_ref, g1_ref, w_ref, wgate_ref, wgm_ref, cw_ref, qg_ref, kg_ref, inv_ref, seg_ref,
                 ya_ref, q_ref, kc_ref, vc_ref, ksl_ref, vsl0_ref, vsl1_ref, kwn_ref, vwn0_ref, vwn1_ref, gate_ref,
                 gm_ref, ubuf, *, tiles_per_seq, tm):
    i = pl.program_id(0)

    @pl.when(i % tiles_per_seq == 0)
    def _():
        ubuf[0:SUBLANES, :] = jnp.zeros((SUBLANES, CONV_W), F32)

    x = x_ref[...]
    ms = jnp.mean(x * x, axis=-1, keepdims=True)
    h = (x * lax.rsqrt(ms + EPS) * g1_ref[...]).astype(BF16)

    pa = _dot(h, w_ref[:, 0:_A_END])
    u = pa[:, CONV_W:2 * CONV_W] * pa[:, 2 * CONV_W:3 * CONV_W]
    ubuf[SUBLANES:SUBLANES + tm, :] = u
    cw = cw_ref[...]
    y = (cw[0:1, :] * ubuf[SUBLANES - 2:SUBLANES - 2 + tm, :]
         + cw[1:2, :] * ubuf[SUBLANES - 1:SUBLANES - 1 + tm, :]
         + cw[2:3, :] * u)
    ya_ref[...] = (pa[:, 0:CONV_W] * y).astype(BF16)
    ubuf[0:SUBLANES, :] = ubuf[tm:tm + SUBLANES, :]

    ang = pos_ref[...] * inv_ref[...]
    lane = lax.broadcasted_iota(jnp.int32, ang.shape, 1)
    cos = jnp.cos(ang)
    sin_signed = jnp.where((lane & (HEAD_DIM // 2)) != 0, jnp.sin(ang), -jnp.sin(ang))

    gm_ref[...] = jax.nn.sigmoid(_dot(h, wgm_ref[...])).astype(BF16)
    pb = _dot(h, w_ref[:, _Q_OFF:_GATE_OFF])
    scale = HEAD_DIM ** -0.5 * LOG2E
    seg = seg_ref[...]
    for c in range(ATTN_W // SEG_W):
        cols = slice(c * SEG_W, (c + 1) * SEG_W)
        qn = _head_norm(pb[:, cols], qg_ref[:, cols], seg)
        for t in range(SEG_W // LANES):
            lanes = slice(t * LANES, (t + 1) * LANES)
            q_ref[:, c * SEG_W + t * LANES:c * SEG_W + (t + 1) * LANES] = (
                _rope(qn[:, lanes], cos, sin_signed) * scale).astype(BF16)
    kv = pb[:, ATTN_W:ATTN_W + 6 * KV_W]
    kc_ref[...] = kv[:, 0:KV_W]
    vc_ref[...] = kv[:, KV_W:2 * KV_W]
    kn = _head_norm(jnp.concatenate([kv[:, 2 * KV_W:3 * KV_W], kv[:, 4 * KV_W:5 * KV_W]], axis=1), kg_ref[...], seg)
    ksl_ref[...] = _rope(kn[:, 0:KV_W], cos, sin_signed).astype(BF16)
    kwn_ref[...] = _rope(kn[:, KV_W:2 * KV_W], cos, sin_signed).astype(BF16)
    first = lane < HEAD_DIM
    for v, out0, out1 in ((kv[:, 3 * KV_W:4 * KV_W], vsl0_ref, vsl1_ref), (kv[:, 5 * KV_W:6 * KV_W], vwn0_ref, vwn1_ref)):
        out0[...] = jnp.where(first, v, 1.0).astype(BF16)
        out1[...] = jnp.where(first, 1.0, v).astype(BF16)
    gate_ref[...] = jax.nn.sigmoid(_dot(h, wgate_ref[...]))


def _proj_call(x2, posf, g1, w_main, w_gate, w_gm, conv_w, qg, kg, inv, seg, seq):
    n = x2.shape[0]
    tm = min(ROW_TILE, seq)
    row = lambda w: pl.BlockSpec((tm, w), lambda i: (i, 0))
    full = lambda a: pl.BlockSpec(a.shape, lambda i: (0,) * a.ndim)
    once = lambda a: pl.BlockSpec(a.shape, lambda i: (0,) * a.ndim, pipeline_mode=pl.Buffered(1))
    out_shapes = (
        jax.ShapeDtypeStruct((n, CONV_W), BF16),
        jax.ShapeDtypeStruct((n, ATTN_W), BF16),
        jax.ShapeDtypeStruct((n, KV_W), F32),
        jax.ShapeDtypeStruct((n, KV_W), F32),
        jax.ShapeDtypeStruct((n, KV_W), BF16),
        jax.ShapeDtypeStruct((n, KV_W), BF16),
        jax.ShapeDtypeStruct((n, KV_W), BF16),
        jax.ShapeDtypeStruct((n, KV_W), BF16),
        jax.ShapeDtypeStruct((n, KV_W), BF16),
        jax.ShapeDtypeStruct((n, KV_W), BF16),
        jax.ShapeDtypeStruct((n, GATE_PAD), F32),
        jax.ShapeDtypeStruct((n, 2 * D_MODEL), BF16),
    )
    return pl.pallas_call(
        functools.partial(_proj_kernel, tiles_per_seq=seq // tm, tm=tm),
        grid=(n // tm,),
        in_specs=[row(D_MODEL), row(1), full(g1), once(w_main), once(w_gate), once(w_gm), full(conv_w), full(qg),
                  full(kg), full(inv), full(seg)],
        out_specs=[row(s.shape[1]) for s in out_shapes],
        out_shape=out_shapes,
        scratch_shapes=[pltpu.VMEM((tm + SUBLANES, CONV_W), F32)],
        compiler_params=pltpu.CompilerParams(dimension_semantics=("arbitrary",), vmem_limit_bytes=VMEM_LIMIT),
        name="proj",
    )(x2, posf, g1, w_main, w_gate, w_gm, conv_w, qg, kg, inv, seg)


def _compress_kernel(kc_ref, vc_ref, pek_ref, pev_ref, w1k_ref, w2k_ref, w1v_ref, w2v_ref, kg_ref, seg_ref,
                     kcmp_ref, vcmp0_ref, vcmp1_ref, buf, *, seq):
    nc = seq // CMP_STRIDE
    buf[seq:seq + CMP_BLOCK, :] = jnp.zeros((CMP_BLOCK, KV_W), F32)
    for src, pe, w1, w2, is_key in ((kc_ref, pek_ref, w1k_ref, w2k_ref, True), (vc_ref, pev_ref, w1v_ref, w2v_ref, False)):
        buf[0:seq, :] = src[0]
        acc = jnp.zeros((nc, KV_W), F32)
        for r in range(CMP_BLOCK):
            xr = buf[pl.ds(r, nc, stride=CMP_STRIDE), :] + pe[r:r + 1, :]
            acc = acc + _dot(xr.astype(BF16), w1[r])
        hmid = acc * jax.nn.sigmoid(acc)
        o = _dot(hmid.astype(BF16), w2[...])
        if is_key:
            kcmp_ref[0] = _head_norm(o, kg_ref[...], seg_ref[0:KV_W, 0:KV_W]).astype(BF16)
        else:
            first = lax.broadcasted_iota(jnp.int32, o.shape, 1) < HEAD_DIM
            vcmp0_ref[0] = jnp.where(first, o, 1.0).astype(BF16)
            vcmp1_ref[0] = jnp.where(first, 1.0, o).astype(BF16)


def _compress_call(kc, vc, pek, pev, w1k, w2k, w1v, w2v, kg0, seg):
    b, seq, _ = kc.shape
    nc = seq // CMP_STRIDE
    per_b = lambda rows: pl.BlockSpec((1, rows, KV_W), lambda i: (i, 0, 0))
    full = lambda a: pl.BlockSpec(a.shape, lambda i: (0,) * a.ndim)
    return pl.pallas_call(
        functools.partial(_compress_kernel, seq=seq),
        grid=(b,),
        in_specs=[per_b(seq), per_b(seq), full(pek), full(pev), full(w1k), full(w2k), full(w1v), full(w2v), full(kg0),
                  full(seg)],
        out_specs=[per_b(nc)] * 3,
        out_shape=(jax.ShapeDtypeStruct((b, nc, KV_W), BF16),) * 3,
        scratch_shapes=[pltpu.VMEM((seq + CMP_BLOCK, KV_W), F32)],
        compiler_params=pltpu.CompilerParams(dimension_semantics=("arbitrary",), vmem_limit_bytes=VMEM_LIMIT),
        name="compress",
    )(kc, vc, pek, pev, w1k, w2k, w1v, w2v, kg0, seg)


def _loop_pairs(lo, hi, step, carry):
    pairs = (hi - lo) // 2

    def two(i, c):
        return step(lo + 2 * i + 1, step(lo + 2 * i, c))

    return lax.fori_loop(lo + 2 * pairs, hi, step, lax.fori_loop(0, pairs, two, carry))


def _nsa_kernel(q_ref, gate_ref, kcmp_ref, vcmp0_ref, vcmp1_ref, ksel_ref, vsel0_ref, vsel1_ref, kwin_ref, vwin0_ref,
                vwin1_ref, ova_ref, hot_ref, yb_ref, *scratch, seq, n_sel, wlen, bounded, qblock):
    qb = pl.program_id(1)
    t0 = qb * qblock
    nc = seq // CMP_STRIDE
    nb = seq // SEL_BLOCK
    rows = HEADS_PER_GROUP * qblock
    groups = range(N_GROUPS)
    vcmp_refs = (vcmp0_ref, vcmp1_ref)
    vsel_refs = (vsel0_ref, vsel1_ref)
    vwin_refs = (vwin0_ref, vwin1_ref)

    q = q_ref[0]
    gates = gate_ref[0]
    lane_q = lax.broadcasted_iota(jnp.int32, (qblock, LANES), 1)
    t_col = t0 + lax.broadcasted_iota(jnp.int32, (qblock, 1), 0)
    rep = lambda a: jnp.concatenate([a] * HEADS_PER_GROUP, axis=0)

    def weights(s):
        if bounded:
            return jnp.exp2(s)
        m = jnp.maximum(jnp.max(s, axis=1, keepdims=True), 0.5 * NEG)
        return jnp.exp2(s - m)

    def normalized(acc):
        l = pltpu.roll(acc, HEAD_DIM, axis=1)
        return acc * jnp.where(l > 0.0, 1.0 / l, 0.0)

    def gated(o, g, branch):
        cols = [N_BRANCH * (g * HEADS_PER_GROUP + hl) + branch for hl in range(HEADS_PER_GROUP)]
        return jnp.concatenate([gates[:, c:c + 1] * o[hl * qblock:(hl + 1) * qblock] for hl, c in enumerate(cols)],
                               axis=0)

    cend = lax.broadcasted_iota(jnp.int32, (1, nc), 1) * CMP_STRIDE + (CMP_BLOCK - 1)
    done = rep(jnp.where(cend <= t_col, 0.0, NEG))

    qs, o_c, o_w, imp_t, lhs = [], [], [], [], []
    for g in groups:
        in_group = (lane_q >= HEAD_DIM * g) & (lane_q < HEAD_DIM * (g + 1))
        parts = []
        for hl in range(HEADS_PER_GROUP):
            pair, side = divmod(hl, 2)
            c0 = g * HEADS_PER_GROUP * HEAD_DIM + pair * LANES
            qp = q[:, c0:c0 + LANES].astype(F32)
            if side != g:
                qp = pltpu.roll(qp, HEAD_DIM, axis=1)
            parts.append(jnp.where(in_group, qp, 0.0).astype(BF16))
        qs.append(jnp.concatenate(parts, axis=0))

        p = weights(_dot_nt(qs[g], kcmp_ref[0]) + done)
        both = _dot(p.astype(BF16), jnp.concatenate([vcmp_refs[g][0], ova_ref[...]], axis=1))
        o_c.append(gated(normalized(both[:, 0:LANES]), g, 0))

        share = normalized(both[:, LANES:2 * LANES])
        imp = share[0:qblock]
        for hl in range(1, HEADS_PER_GROUP):
            imp = imp + share[hl * qblock:(hl + 1) * qblock]
        imp_t.append(imp.T[0:nb])

    def block_mask(imp_g, live):
        dead = [jnp.full((nb - live, qblock), NEG, F32)] if live < nb else []
        dead.append(jnp.zeros((LANES - nb, qblock), F32))
        if live <= n_sel:
            return jnp.concatenate([jnp.zeros((live, qblock), F32)] + dead, axis=0)
        blk = lax.broadcasted_iota(jnp.int32, (live, qblock), 0)
        cur = (t0 + lax.broadcasted_iota(jnp.int32, (live, qblock), 1)) // SEL_BLOCK
        forced = (blk == 0) | (blk == cur) | (blk == cur - 1)
        imp_a = jnp.where(forced, FORCE, jnp.where(blk <= cur, imp_g[0:live], -FORCE))
        tiles = [imp_a[r * SUBLANES:(r + 1) * SUBLANES] for r in range(live // SUBLANES)]
        sub = lax.broadcasted_iota(jnp.int32, (SUBLANES, qblock), 0)
        rank = [jnp.zeros((SUBLANES, qblock), jnp.int32) for _ in tiles]
        for jp in range(live):
            other = jnp.broadcast_to(imp_a[jp:jp + 1, :], (SUBLANES, qblock))
            for r, tile in enumerate(tiles):
                if r > jp // SUBLANES:
                    before = other >= tile
                elif r < jp // SUBLANES:
                    before = other > tile
                else:
                    before = (other > tile) | ((other == tile) & (sub > jp % SUBLANES))
                rank[r] = rank[r] + before.astype(jnp.int32)
        return jnp.concatenate([jnp.where(r < n_sel, 0.0, NEG) for r in rank] + dead, axis=0)

    live_steps = list(range(n_sel, nb + 1, n_sel))
    branches = [functools.partial(lambda live, *imps: tuple(block_mask(i, live) for i in imps), live)
                for live in live_steps]
    neg_t = lax.switch(((t0 + qblock - 1) // SEL_BLOCK) // n_sel, branches, *imp_t)
    for g in groups:
        lhs.append(jnp.concatenate([qs[g], rep(neg_t[g].T.astype(BF16))], axis=1))

    def scores(kt):
        k0 = pl.multiple_of(kt * KEY_TILE, KEY_TILE)
        rhs = jnp.concatenate([ksel_ref[0, pl.ds(k0, KEY_TILE), :], hot_ref[pl.ds(k0, KEY_TILE), :]], axis=1)
        kpos = k0 + lax.broadcasted_iota(jnp.int32, (1, KEY_TILE), 1)
        causal = rep(jnp.where(kpos <= t_col, 0.0, NEG))
        return [_dot_nt(lhs[g], rhs) + causal for g in groups]

    def values(kt, g):
        return vsel_refs[g][0, pl.ds(pl.multiple_of(kt * KEY_TILE, KEY_TILE), KEY_TILE), :]

    n_tiles = (t0 + qblock - 1) // KEY_TILE + 1
    zeros = tuple(jnp.zeros((rows, LANES), F32) for _ in groups)

    if bounded:
        def sweep(kt, acc):
            s_t = scores(kt)
            return tuple(acc[g] + _dot(jnp.exp2(s_t[g]).astype(BF16), values(kt, g)) for g in groups)

        acc_s = _loop_pairs(0, n_tiles, sweep, zeros)
    else:
        sbuf, = scratch

        lane_tiles = [slice(c * LANES, (c + 1) * LANES) for c in range(KEY_TILE // LANES)]

        def fold_max(m_run, s_t):
            return functools.reduce(jnp.maximum, [s_t[:, c] for c in lane_tiles], m_run)

        def pass1(kt, m_run):
            s_t = scores(kt)
            for g in groups:
                sbuf[g, kt] = s_t[g]
            return tuple(fold_max(m_run[g], s_t[g]) for g in groups)

        m_run = _loop_pairs(0, n_tiles, pass1, tuple(jnp.full((rows, LANES), NEG, F32) for _ in groups))
        m_sel = [jnp.broadcast_to(jnp.max(m_run[g], axis=1, keepdims=True), (rows, LANES)) for g in groups]

        def pass2(kt, acc):
            out = []
            for g in groups:
                s_g = sbuf[g, kt]
                p = jnp.concatenate([jnp.exp2(s_g[:, c] - m_sel[g]) for c in lane_tiles], axis=1).astype(BF16)
                out.append(acc[g] + _dot(p, values(kt, g)))
            return tuple(out)

        acc_s = _loop_pairs(0, n_tiles, pass2, zeros)

    start = pl.multiple_of(jnp.maximum(t0 + qblock - wlen, 0), qblock)
    diff = t_col - (start + lax.broadcasted_iota(jnp.int32, (1, wlen), 1))
    band = rep(jnp.where((diff >= 0) & (diff < WINDOW), 0.0, NEG))
    kw = kwin_ref[0, pl.ds(start, wlen), :]
    for g in groups:
        p = weights(_dot_nt(qs[g], kw) + band)
        o_w.append(gated(normalized(_dot(p.astype(BF16), vwin_refs[g][0, pl.ds(start, wlen), :])), g, 2))

    for g in groups:
        out = o_c[g] + gated(normalized(acc_s[g]), g, 1) + o_w[g]
        for pair in range(HEADS_PER_GROUP // 2):
            left = out[2 * pair * qblock:(2 * pair + 1) * qblock]
            right = out[(2 * pair + 1) * qblock:(2 * pair + 2) * qblock]
            if g == 0:
                right = pltpu.roll(right, HEAD_DIM, axis=1)
            else:
                left = pltpu.roll(left, HEAD_DIM, axis=1)
            c0 = g * HEADS_PER_GROUP * HEAD_DIM + pair * LANES
            yb_ref[0, :, c0:c0 + LANES] = jnp.where(lane_q < HEAD_DIM, left, right).astype(BF16)


def _nsa_call(bounded, q, gates, kcmp, vcmp0, vcmp1, ksl, vsl0, vsl1, kwn, vwn0, vwn1, ova, hot):
    b, seq, _ = q.shape
    nc = seq // CMP_STRIDE
    qblock = min(SWEEP_Q_BLOCK if bounded else Q_BLOCK, seq)
    wlen = min(WINDOW + qblock, seq)
    n_sel = min(N_SEL, seq // SEL_BLOCK)
    qblk = lambda w: pl.BlockSpec((1, qblock, w), lambda i, j: (i, j, 0))
    per_b = lambda rows: pl.BlockSpec((1, rows, KV_W), lambda i, j: (i, 0, 0))
    full = lambda a: pl.BlockSpec(a.shape, lambda i, j: (0,) * a.ndim)
    score_buf = pltpu.VMEM((N_GROUPS, seq // KEY_TILE, HEADS_PER_GROUP * qblock, KEY_TILE), F32)
    return pl.pallas_call(
        functools.partial(_nsa_kernel, seq=seq, n_sel=n_sel, wlen=wlen, bounded=bounded, qblock=qblock),
        grid=(b, seq // qblock),
        in_specs=[qblk(ATTN_W), qblk(GATE_PAD)] + [per_b(nc)] * 3 + [per_b(seq)] * 6 + [full(ova), full(hot)],
        out_specs=qblk(ATTN_W),
        out_shape=jax.ShapeDtypeStruct((b, seq, ATTN_W), BF16),
        scratch_shapes=[] if bounded else [score_buf],
        compiler_params=pltpu.CompilerParams(dimension_semantics=("arbitrary", "arbitrary"),
                                             vmem_limit_bytes=VMEM_LIMIT),
        name="nsa_bounded" if bounded else "nsa",
    )(q, gates, kcmp, vcmp0, vcmp1, ksl, vsl0, vsl1, kwn, vwn0, vwn1, ova, hot)


_FF_CHUNKS = ((0, 1024), (1024, 1024), (2048, 768))


def _mix_ffn_kernel(x_ref, ya_ref, yb_ref, gm_ref, wa_ref, wb_ref, wo_ref, g2_ref, win_ref, cw_ref, cb_ref, wout_ref,
                    out_ref, gbuf, *, tiles_per_seq, tm):
    i = pl.program_id(0)

    @pl.when(i % tiles_per_seq == 0)
    def _():
        gbuf[0:SUBLANES, :] = jnp.zeros((SUBLANES, D_FF), F32)

    gm = gm_ref[...].astype(F32)
    mix = gm[:, 0:D_MODEL] * _dot(ya_ref[...], wa_ref[...]) + gm[:, D_MODEL:] * _dot(yb_ref[...], wb_ref[...])
    x = x_ref[...] + _dot(mix.astype(BF16), wo_ref[...])

    ms = jnp.mean(x * x, axis=-1, keepdims=True)
    h = (x * lax.rsqrt(ms + EPS) * g2_ref[...]).astype(BF16)
    acc = jnp.zeros((tm, D_MODEL), F32)
    for c0, w in _FF_CHUNKS:
        cols = slice(c0, c0 + w)
        gpre = _dot(h, win_ref[:, c0:c0 + w])
        up = _dot(h, win_ref[:, D_FF + c0:D_FF + c0 + w])
        gbuf[SUBLANES:SUBLANES + tm, cols] = gpre
        cw = cw_ref[:, cols]
        conv = (cw[0:1, :] * gbuf[SUBLANES - 2:SUBLANES - 2 + tm, cols]
                + cw[1:2, :] * gbuf[SUBLANES - 1:SUBLANES - 1 + tm, cols]
                + cw[2:3, :] * gpre) + cb_ref[:, cols]
        gbuf[0:SUBLANES, cols] = gbuf[tm:tm + SUBLANES, cols]
        act = conv * jax.nn.sigmoid(conv) * up
        acc = acc + _dot(act.astype(BF16), wout_ref[c0:c0 + w, :])
    out_ref[...] = x + acc


def _mix_ffn_call(x2, ya, yb, gm, wa, wb, wo, g2, win, conv_w, conv_b, wout, seq):
    n = x2.shape[0]
    tm = min(ROW_TILE, seq)
    row = lambda w: pl.BlockSpec((tm, w), lambda i: (i, 0))
    full = lambda a: pl.BlockSpec(a.shape, lambda i: (0,) * a.ndim)
    once = lambda a: pl.BlockSpec(a.shape, lambda i: (0,) * a.ndim, pipeline_mode=pl.Buffered(1))
    return pl.pallas_call(
        functools.partial(_mix_ffn_kernel, tiles_per_seq=seq // tm, tm=tm),
        grid=(n // tm,),
        in_specs=[row(D_MODEL), row(CONV_W), row(ATTN_W), row(2 * D_MODEL), once(wa), once(wb), once(wo), full(g2),
                  once(win), full(conv_w), full(conv_b), once(wout)],
        out_specs=row(D_MODEL),
        out_shape=jax.ShapeDtypeStruct((n, D_MODEL), F32),
        scratch_shapes=[pltpu.VMEM((tm + SUBLANES, D_FF), F32)],
        compiler_params=pltpu.CompilerParams(dimension_semantics=("arbitrary",), vmem_limit_bytes=VMEM_LIMIT),
        name="mix_ffn",
    )(x2, ya, yb, gm, wa, wb, wo, g2, win, conv_w, conv_b, wout)


def _block_diag2(w):
    z = jnp.zeros_like(w)
    return jnp.concatenate([jnp.concatenate([w, z], axis=-1), jnp.concatenate([z, w], axis=-1)], axis=-2)


def _static_tables(seq):
    nc, nb = seq // CMP_STRIDE, seq // SEL_BLOCK
    n_cmp = (seq - CMP_BLOCK) // CMP_STRIDE + 1
    ii = np.arange(nc)[None, :] * CMP_STRIDE
    jj = np.arange(nb)[:, None] * SEL_BLOCK
    ovt = ((ii < jj + SEL_BLOCK) & (ii + CMP_BLOCK > jj) & (np.arange(nc)[None, :] < n_cmp)).astype(np.float32)
    ova = np.ones((nc, LANES), np.float32)
    ova[:, :HEAD_DIM] = 0.0
    ova[:, :nb] = ovt.T
    hot = (np.arange(LANES)[None, :] == (np.arange(seq) // SEL_BLOCK)[:, None]).astype(np.float32)
    head = np.arange(SEG_W) // HEAD_DIM
    seg = (head[:, None] == head[None, :]).astype(np.float32)
    return jnp.asarray(ova, BF16), jnp.asarray(hot, BF16), jnp.asarray(seg, BF16)


def kernel(x, positions, norm1_g, w_in, conv_a_w, w_a_out, q_norm_g, k_norm_g, cmp_k_pe, cmp_k_w1, cmp_k_w2, cmp_v_pe, cmp_v_w1, cmp_v_w2, w_b_out, w_o, norm2_g, w_ffn_in, ffn_conv_w, ffn_conv_b, w_ffn_out):
    b, seq, d = x.shape
    n = b * seq
    half = HEAD_DIM // 2
    inv = ROPE_THETA ** (-jnp.arange(half, dtype=F32) / half)
    inv_lanes = jnp.tile(inv, LANES // half)[None, :]
    posf = positions.astype(F32).reshape(n, 1)
    ova, hot, seg = _static_tables(seq)
    gate_cols = N_BRANCH * N_HEADS

    x2 = x.reshape(n, d)
    for l in range(w_in.shape[0]):
        w = w_in[l]
        w_main = w[:, :_GATE_OFF].astype(BF16)
        w_gate = jnp.pad(w[:, _GATE_OFF:_GATE_OFF + gate_cols], ((0, 0), (0, GATE_PAD - gate_cols))).astype(BF16)
        w_gm = w[:, _GATE_OFF + gate_cols:].astype(BF16)
        qg = jnp.tile(q_norm_g[l], N_HEADS)[None, :]
        kg = jnp.tile(k_norm_g[l], (1, N_GROUPS))
        ya, q, kc, vc, ksl, vsl0, vsl1, kwn, vwn0, vwn1, gates, gm = _proj_call(
            x2, posf, norm1_g[l][None, :], w_main, w_gate, w_gm, conv_a_w[l], qg, kg[1:3].reshape(1, 2 * KV_W),
            inv_lanes, seg, seq)

        per_b = lambda a: a.reshape(b, seq, a.shape[-1])
        w1_blocks = lambda w1: _block_diag2(w1.reshape(CMP_BLOCK, HEAD_DIM, HEAD_DIM)).astype(BF16)
        kcmp, vcmp0, vcmp1 = _compress_call(
            per_b(kc), per_b(vc), jnp.tile(cmp_k_pe[l], (1, N_GROUPS)), jnp.tile(cmp_v_pe[l], (1, N_GROUPS)),
            w1_blocks(cmp_k_w1[l]), _block_diag2(cmp_k_w2[l]).astype(BF16),
            w1_blocks(cmp_v_w1[l]), _block_diag2(cmp_v_w2[l]).astype(BF16), kg[0:1], seg)

        reach = HEAD_DIM * (HEAD_DIM ** -0.5 * LOG2E) * jnp.max(jnp.abs(q_norm_g[l])) * jnp.max(jnp.abs(k_norm_g[l]))
        nsa_args = (per_b(q), per_b(gates), kcmp, vcmp0, vcmp1, per_b(ksl), per_b(vsl0), per_b(vsl1), per_b(kwn),
                    per_b(vwn0), per_b(vwn1), ova, hot)
        yb = lax.cond(reach * 1.02 <= SCORE_BOUND, functools.partial(_nsa_call, True),
                      functools.partial(_nsa_call, False), *nsa_args)

        x2 = _mix_ffn_call(x2, ya, yb.reshape(n, ATTN_W), gm, w_a_out[l].astype(BF16), w_b_out[l].astype(BF16),
                           w_o[l].astype(BF16), norm2_g[l][None, :], w_ffn_in[l].astype(BF16), ffn_conv_w[l],
                           ffn_conv_b[l][None, :], w_ffn_out[l].astype(BF16), seq)
    return x2.reshape(b, seq, d)
```

```python
import functools

import jax
import jax.numpy as jnp
import numpy as np
from jax import lax
from jax.experimental import pallas as pl
from jax.experimental.pallas import tpu as pltpu

F32 = jnp.float32
BF16 = jnp.bfloat16

D_MODEL = 1024
CONV_W = 512
CONV_K = 3
N_HEADS = 8
HEAD_DIM = 64
N_GROUPS = 2
HEADS_PER_GROUP = N_HEADS // N_GROUPS
ATTN_W = N_HEADS * HEAD_DIM
KV_W = N_GROUPS * HEAD_DIM
CMP_BLOCK = 32
CMP_STRIDE = 16
SEL_BLOCK = 64
N_SEL = 16
WINDOW = 512
Q_BLOCK = 128
SWEEP_Q_BLOCK = 256
N_BRANCH = 3
ROPE_THETA = 10000.0
D_FF = 2816
EPS = 1e-6
NEG = -1e30
FORCE = 1e9
LOG2E = 1.4426950408889634
SCORE_BOUND = 60.0

LANES = 128
SUBLANES = 8
KEY_TILE = 512
SEG_W = 256
ROW_TILE = 512
GATE_PAD = LANES
VMEM_LIMIT = 56 * 1024 * 1024

_A_END = 3 * CONV_W
_Q_OFF = _A_END
_KV_OFF = _Q_OFF + ATTN_W
_GATE_OFF = _KV_OFF + 6 * KV_W


def _dot(a, b):
    return jnp.dot(a, b, preferred_element_type=F32)


def _dot_nt(a, b):
    return lax.dot_general(a, b, (((1,), (1,)), ((), ())), preferred_element_type=F32)


def _head_norm(x, gain, seg):
    sq = x * x
    hi = sq.astype(BF16)
    lo = (sq - hi.astype(F32)).astype(BF16)
    ms = (_dot(hi, seg) + _dot(lo, seg)) * (1.0 / HEAD_DIM)
    return x * lax.rsqrt(ms + EPS) * gain


def _rope(x, cos, sin_signed):
    lane = lax.broadcasted_iota(jnp.int32, x.shape, 1)
    half = HEAD_DIM // 2
    rot = jnp.where((lane & half) != 0, pltpu.roll(x, half, axis=1), pltpu.roll(x, LANES - half, axis=1))
    return x * cos + rot * sin_signed


def _proj_kernel(x_ref, pos_ref, g1_ref, w_ref, wgate_ref, wgm_ref, cw_ref, qg_ref, kg_ref, inv_ref, rcos_ref,
                 rsin_ref, seg_ref,
                 ya_ref, q_ref, kc_ref, vc_ref, ksl_ref, vsl0_ref, vsl1_ref, kwn_ref, vwn0_ref, vwn1_ref, gate_ref,
                 gm_ref, ubuf, *, tiles_per_seq, tm, stepwise):
    i = pl.program_id(0)

    @pl.when(i % tiles_per_seq == 0)
    def _():
        ubuf[0:SUBLANES, :] = jnp.zeros((SUBLANES, CONV_W), F32)

    x = x_ref[...]
    ms = jnp.mean(x * x, axis=-1, keepdims=True)
    h = (x * lax.rsqrt(ms + EPS) * g1_ref[...]).astype(BF16)

    pa = _dot(h, w_ref[:, 0:_A_END])
    u = pa[:, CONV_W:2 * CONV_W] * pa[:, 2 * CONV_W:3 * CONV_W]
    ubuf[SUBLANES:SUBLANES + tm, :] = u
    cw = cw_ref[...]
    y = (cw[0:1, :] * ubuf[SUBLANES - 2:SUBLANES - 2 + tm, :]
         + cw[1:2, :] * ubuf[SUBLANES - 1:SUBLANES - 1 + tm, :]
         + cw[2:3, :] * u)
    ya_ref[...] = (pa[:, 0:CONV_W] * y).astype(BF16)
    ubuf[0:SUBLANES, :] = ubuf[tm:tm + SUBLANES, :]

    pos = pos_ref[0]
    inv = inv_ref[...]
    if stepwise:
        p0 = pos[0:1, 0:1]
        ca, sa = jnp.cos(p0 * inv), jnp.sin(p0 * inv)
        cb, sb = rcos_ref[...], rsin_ref[...]
        cos, sin = ca * cb - sa * sb, sa * cb + ca * sb
    else:
        cols = [jnp.broadcast_to(pos[c:c + 1, :], (LANES, LANES)).T for c in range(tm // LANES)]
        ang = jnp.concatenate(cols, axis=0) * inv
        cos, sin = jnp.cos(ang), jnp.sin(ang)
    lane = lax.broadcasted_iota(jnp.int32, cos.shape, 1)
    sin_signed = jnp.where((lane & (HEAD_DIM // 2)) != 0, sin, -sin)

    gm_ref[...] = jax.nn.sigmoid(_dot(h, wgm_ref[...])).astype(BF16)
    pb = _dot(h, w_ref[:, _Q_OFF:_GATE_OFF])
    scale = HEAD_DIM ** -0.5 * LOG2E
    seg = seg_ref[...]
    for c in range(ATTN_W // SEG_W):
        cols = slice(c * SEG_W, (c + 1) * SEG_W)
        qn = _head_norm(pb[:, cols], qg_ref[:, cols], seg)
        for t in range(SEG_W // LANES):
            lanes = slice(t * LANES, (t + 1) * LANES)
            q_ref[:, c * SEG_W + t * LANES:c * SEG_W + (t + 1) * LANES] = (
                _rope(qn[:, lanes], cos, sin_signed) * scale).astype(BF16)
    kv = pb[:, ATTN_W:ATTN_W + 6 * KV_W]
    kc_ref[...] = kv[:, 0:KV_W]
    vc_ref[...] = kv[:, KV_W:2 * KV_W]
    kn = _head_norm(jnp.concatenate([kv[:, 2 * KV_W:3 * KV_W], kv[:, 4 * KV_W:5 * KV_W]], axis=1), kg_ref[...], seg)
    ksl_ref[...] = _rope(kn[:, 0:KV_W], cos, sin_signed).astype(BF16)
    kwn_ref[...] = _rope(kn[:, KV_W:2 * KV_W], cos, sin_signed).astype(BF16)
    first = lane < HEAD_DIM
    for v, out0, out1 in ((kv[:, 3 * KV_W:4 * KV_W], vsl0_ref, vsl1_ref), (kv[:, 5 * KV_W:6 * KV_W], vwn0_ref, vwn1_ref)):
        out0[...] = jnp.where(first, v, 1.0).astype(BF16)
        out1[...] = jnp.where(first, 1.0, v).astype(BF16)
    gate_ref[...] = jax.nn.sigmoid(_dot(h, wgate_ref[...]))


def _proj_call(stepwise, seq, x2, pos3, g1, w_main, w_gate, w_gm, conv_w, qg, kg, inv, rcos, rsin, seg):
    n = x2.shape[0]
    tm = min(ROW_TILE, seq)
    row = lambda w: pl.BlockSpec((tm, w), lambda i: (i, 0))
    full = lambda a: pl.BlockSpec(a.shape, lambda i: (0,) * a.ndim)
    once = lambda a: pl.BlockSpec(a.shape, lambda i: (0,) * a.ndim, pipeline_mode=pl.Buffered(1))
    out_shapes = (
        jax.ShapeDtypeStruct((n, CONV_W), BF16),
        jax.ShapeDtypeStruct((n, ATTN_W), BF16),
        jax.ShapeDtypeStruct((n, KV_W), F32),
        jax.ShapeDtypeStruct((n, KV_W), F32),
        jax.ShapeDtypeStruct((n, KV_W), BF16),
        jax.ShapeDtypeStruct((n, KV_W), BF16),
        jax.ShapeDtypeStruct((n, KV_W), BF16),
        jax.ShapeDtypeStruct((n, KV_W), BF16),
        jax.ShapeDtypeStruct((n, KV_W), BF16),
        jax.ShapeDtypeStruct((n, KV_W), BF16),
        jax.ShapeDtypeStruct((n, GATE_PAD), F32),
        jax.ShapeDtypeStruct((n, 2 * D_MODEL), BF16),
    )
    return pl.pallas_call(
        functools.partial(_proj_kernel, tiles_per_seq=seq // tm, tm=tm, stepwise=stepwise),
        grid=(n // tm,),
        in_specs=[row(D_MODEL), pl.BlockSpec((1, tm // LANES, LANES), lambda i: (i, 0, 0)), full(g1), once(w_main),
                  once(w_gate), once(w_gm), full(conv_w), full(qg), full(kg), full(inv), full(rcos), full(rsin), full(seg)],
        out_specs=[row(s.shape[1]) for s in out_shapes],
        out_shape=out_shapes,
        scratch_shapes=[pltpu.VMEM((tm + SUBLANES, CONV_W), F32)],
        compiler_params=pltpu.CompilerParams(dimension_semantics=("arbitrary",), vmem_limit_bytes=VMEM_LIMIT),
        name="proj_stepwise" if stepwise else "proj",
    )(x2, pos3, g1, w_main, w_gate, w_gm, conv_w, qg, kg, inv, rcos, rsin, seg)


def _compress_kernel(kc_ref, vc_ref, pek_ref, pev_ref, w1k_ref, w2k_ref, w1v_ref, w2v_ref, kg_ref, seg_ref,
                     kcmp_ref, vcmp0_ref, vcmp1_ref, buf, *, seq):
    nc = seq // CMP_STRIDE
    buf[seq:seq + CMP_BLOCK, :] = jnp.zeros((CMP_BLOCK, KV_W), F32)
    for src, pe, w1, w2, is_key in ((kc_ref, pek_ref, w1k_ref, w2k_ref, True), (vc_ref, pev_ref, w1v_ref, w2v_ref, False)):
        buf[0:seq, :] = src[0]
        acc = jnp.zeros((nc, KV_W), F32)
        for r in range(CMP_BLOCK):
            xr = buf[pl.ds(r, nc, stride=CMP_STRIDE), :] + pe[r:r + 1, :]
            acc = acc + _dot(xr.astype(BF16), w1[r])
        hmid = acc * jax.nn.sigmoid(acc)
        o = _dot(hmid.astype(BF16), w2[...])
        if is_key:
            kcmp_ref[0] = _head_norm(o, kg_ref[...], seg_ref[0:KV_W, 0:KV_W]).astype(BF16)
        else:
            first = lax.broadcasted_iota(jnp.int32, o.shape, 1) < HEAD_DIM
            vcmp0_ref[0] = jnp.where(first, o, 1.0).astype(BF16)
            vcmp1_ref[0] = jnp.where(first, 1.0, o).astype(BF16)


def _compress_call(kc, vc, pek, pev, w1k, w2k, w1v, w2v, kg0, seg):
    b, seq, _ = kc.shape
    nc = seq // CMP_STRIDE
    per_b = lambda rows: pl.BlockSpec((1, rows, KV_W), lambda i: (i, 0, 0))
    full = lambda a: pl.BlockSpec(a.shape, lambda i: (0,) * a.ndim)
    return pl.pallas_call(
        functools.partial(_compress_kernel, seq=seq),
        grid=(b,),
        in_specs=[per_b(seq), per_b(seq), full(pek), full(pev), full(w1k), full(w2k), full(w1v), full(w2v), full(kg0),
                  full(seg)],
        out_specs=[per_b(nc)] * 3,
        out_shape=(jax.ShapeDtypeStruct((b, nc, KV_W), BF16),) * 3,
        scratch_shapes=[pltpu.VMEM((seq + CMP_BLOCK, KV_W), F32)],
        compiler_params=pltpu.CompilerParams(dimension_semantics=("arbitrary",), vmem_limit_bytes=VMEM_LIMIT),
        name="compress",
    )(kc, vc, pek, pev, w1k, w2k, w1v, w2v, kg0, seg)


def _loop_pairs(lo, hi, step, carry):
    pairs = (hi - lo) // 2

    def two(i, c):
        return step(lo + 2 * i + 1, step(lo + 2 * i, c))

    return lax.fori_loop(lo + 2 * pairs, hi, step, lax.fori_loop(0, pairs, two, carry))


def _nsa_kernel(q_ref, gate_ref, kcmp_ref, vcmp0_ref, vcmp1_ref, ksel_ref, vsel0_ref, vsel1_ref, kwin_ref, vwin0_ref,
                vwin1_ref, ova_ref, hot_ref, yb_ref, *scratch, seq, n_sel, wlen, bounded, qblock):
    qb = pl.program_id(1)
    t0 = qb * qblock
    nc = seq // CMP_STRIDE
    nb = seq // SEL_BLOCK
    rows = HEADS_PER_GROUP * qblock
    groups = range(N_GROUPS)
    vcmp_refs = (vcmp0_ref, vcmp1_ref)
    vsel_refs = (vsel0_ref, vsel1_ref)
    vwin_refs = (vwin0_ref, vwin1_ref)

    q = q_ref[0]
    gates = gate_ref[0]
    lane_q = lax.broadcasted_iota(jnp.int32, (qblock, LANES), 1)
    t_col = t0 + lax.broadcasted_iota(jnp.int32, (qblock, 1), 0)
    rep = lambda a: jnp.concatenate([a] * HEADS_PER_GROUP, axis=0)

    def weights(s):
        if bounded:
            return jnp.exp2(s)
        m = jnp.maximum(jnp.max(s, axis=1, keepdims=True), 0.5 * NEG)
        return jnp.exp2(s - m)

    def normalized(acc):
        l = pltpu.roll(acc, HEAD_DIM, axis=1)
        return acc * jnp.where(l > 0.0, 1.0 / l, 0.0)

    def gated(o, g, branch):
        cols = [N_BRANCH * (g * HEADS_PER_GROUP + hl) + branch for hl in range(HEADS_PER_GROUP)]
        return jnp.concatenate([gates[:, c:c + 1] * o[hl * qblock:(hl + 1) * qblock] for hl, c in enumerate(cols)],
                               axis=0)

    cend = lax.broadcasted_iota(jnp.int32, (1, nc), 1) * CMP_STRIDE + (CMP_BLOCK - 1)
    done = rep(jnp.where(cend <= t_col, 0.0, NEG))

    qs, o_c, o_w, imp_t, lhs = [], [], [], [], []
    for g in groups:
        in_group = (lane_q >= HEAD_DIM * g) & (lane_q < HEAD_DIM * (g + 1))
        parts = []
        for hl in range(HEADS_PER_GROUP):
            pair, side = divmod(hl, 2)
            c0 = g * HEADS_PER_GROUP * HEAD_DIM + pair * LANES
            qp = q[:, c0:c0 + LANES].astype(F32)
            if side != g:
                qp = pltpu.roll(qp, HEAD_DIM, axis=1)
            parts.append(jnp.where(in_group, qp, 0.0).astype(BF16))
        qs.append(jnp.concatenate(parts, axis=0))

        p = weights(_dot_nt(qs[g], kcmp_ref[0]) + done)
        both = _dot(p.astype(BF16), jnp.concatenate([vcmp_refs[g][0], ova_ref[...]], axis=1))
        o_c.append(gated(normalized(both[:, 0:LANES]), g, 0))

        share = normalized(both[:, LANES:2 * LANES])
        imp = share[0:qblock]
        for hl in range(1, HEADS_PER_GROUP):
            imp = imp + share[hl * qblock:(hl + 1) * qblock]
        imp_t.append(imp.T[0:nb])

    def block_mask(imp_g, live):
        dead = [jnp.full((nb - live, qblock), NEG, F32)] if live < nb else []
        dead.append(jnp.zeros((LANES - nb, qblock), F32))
        if live <= n_sel:
            return jnp.concatenate([jnp.zeros((live, qblock), F32)] + dead, axis=0)
        blk = lax.broadcasted_iota(jnp.int32, (live, qblock), 0)
        cur = (t0 + lax.broadcasted_iota(jnp.int32, (live, qblock), 1)) // SEL_BLOCK
        forced = (blk == 0) | (blk == cur) | (blk == cur - 1)
        imp_a = jnp.where(forced, FORCE, jnp.where(blk <= cur, imp_g[0:live], -FORCE))
        tiles = [imp_a[r * SUBLANES:(r + 1) * SUBLANES] for r in range(live // SUBLANES)]
        sub = lax.broadcasted_iota(jnp.int32, (SUBLANES, qblock), 0)
        rank = [jnp.zeros((SUBLANES, qblock), jnp.int32) for _ in tiles]
        for jp in range(live):
            other = jnp.broadcast_to(imp_a[jp:jp + 1, :], (SUBLANES, qblock))
            for r, tile in enumerate(tiles):
                if r > jp // SUBLANES:
                    before = other >= tile
                elif r < jp // SUBLANES:
                    before = other > tile
                else:
                    before = (other > tile) | ((other == tile) & (sub > jp % SUBLANES))
                rank[r] = rank[r] + before.astype(jnp.int32)
        return jnp.concatenate([jnp.where(r < n_sel, 0.0, NEG) for r in rank] + dead, axis=0)

    live_steps = list(range(n_sel, nb + 1, n_sel))
    branches = [functools.partial(lambda live, *imps: tuple(block_mask(i, live) for i in imps), live)
                for live in live_steps]
    neg_t = lax.switch(((t0 + qblock - 1) // SEL_BLOCK) // n_sel, branches, *imp_t)
    for g in groups:
        lhs.append(jnp.concatenate([qs[g], rep(neg_t[g].T.astype(BF16))], axis=1))

    def scores(kt):
        k0 = pl.multiple_of(kt * KEY_TILE, KEY_TILE)
        rhs = jnp.concatenate([ksel_ref[0, pl.ds(k0, KEY_TILE), :], hot_ref[pl.ds(k0, KEY_TILE), :]], axis=1)
        kpos = k0 + lax.broadcasted_iota(jnp.int32, (1, KEY_TILE), 1)
        causal = rep(jnp.where(kpos <= t_col, 0.0, NEG))
        return [_dot_nt(lhs[g], rhs) + causal for g in groups]

    def values(kt, g):
        return vsel_refs[g][0, pl.ds(pl.multiple_of(kt * KEY_TILE, KEY_TILE), KEY_TILE), :]

    n_tiles = (t0 + qblock - 1) // KEY_TILE + 1
    zeros = tuple(jnp.zeros((rows, LANES), F32) for _ in groups)

    if bounded:
        def sweep(kt, acc):
            s_t = scores(kt)
            return tuple(acc[g] + _dot(jnp.exp2(s_t[g]).astype(BF16), values(kt, g)) for g in groups)

        acc_s = _loop_pairs(0, n_tiles, sweep, zeros)
    else:
        sbuf, = scratch

        lane_tiles = [slice(c * LANES, (c + 1) * LANES) for c in range(KEY_TILE // LANES)]

        def fold_max(m_run, s_t):
            return functools.reduce(jnp.maximum, [s_t[:, c] for c in lane_tiles], m_run)

        def pass1(kt, m_run):
            s_t = scores(kt)
            for g in groups:
                sbuf[g, kt] = s_t[g]
            return tuple(fold_max(m_run[g], s_t[g]) for g in groups)

        m_run = _loop_pairs(0, n_tiles, pass1, tuple(jnp.full((rows, LANES), NEG, F32) for _ in groups))
        m_sel = [jnp.broadcast_to(jnp.max(m_run[g], axis=1, keepdims=True), (rows, LANES)) for g in groups]

        def pass2(kt, acc):
            out = []
            for g in groups:
                s_g = sbuf[g, kt]
                p = jnp.concatenate([jnp.exp2(s_g[:, c] - m_sel[g]) for c in lane_tiles], axis=1).astype(BF16)
                out.append(acc[g] + _dot(p, values(kt, g)))
            return tuple(out)

        acc_s = _loop_pairs(0, n_tiles, pass2, zeros)

    start = pl.multiple_of(jnp.maximum(t0 + qblock - wlen, 0), qblock)
    diff = t_col - (start + lax.broadcasted_iota(jnp.int32, (1, wlen), 1))
    band = rep(jnp.where((diff >= 0) & (diff < WINDOW), 0.0, NEG))
    kw = kwin_ref[0, pl.ds(start, wlen), :]
    for g in groups:
        p = weights(_dot_nt(qs[g], kw) + band)
        o_w.append(gated(normalized(_dot(p.astype(BF16), vwin_refs[g][0, pl.ds(start, wlen), :])), g, 2))

    for g in groups:
        out = o_c[g] + gated(normalized(acc_s[g]), g, 1) + o_w[g]
        for pair in range(HEADS_PER_GROUP // 2):
            left = out[2 * pair * qblock:(2 * pair + 1) * qblock]
            right = out[(2 * pair + 1) * qblock:(2 * pair + 2) * qblock]
            if g == 0:
                right = pltpu.roll(right, HEAD_DIM, axis=1)
            else:
                left = pltpu.roll(left, HEAD_DIM, axis=1)
            c0 = g * HEADS_PER_GROUP * HEAD_DIM + pair * LANES
            yb_ref[0, :, c0:c0 + LANES] = jnp.where(lane_q < HEAD_DIM, left, right).astype(BF16)


def _nsa_call(bounded, q, gates, kcmp, vcmp0, vcmp1, ksl, vsl0, vsl1, kwn, vwn0, vwn1, ova, hot):
    b, seq, _ = q.shape
    nc = seq // CMP_STRIDE
    qblock = min(SWEEP_Q_BLOCK if bounded else Q_BLOCK, seq)
    wlen = min(WINDOW + qblock, seq)
    n_sel = min(N_SEL, seq // SEL_BLOCK)
    qblk = lambda w: pl.BlockSpec((1, qblock, w), lambda i, j: (i, j, 0))
    per_b = lambda rows: pl.BlockSpec((1, rows, KV_W), lambda i, j: (i, 0, 0))
    full = lambda a: pl.BlockSpec(a.shape, lambda i, j: (0,) * a.ndim)
    score_buf = pltpu.VMEM((N_GROUPS, seq // KEY_TILE, HEADS_PER_GROUP * qblock, KEY_TILE), F32)
    return pl.pallas_call(
        functools.partial(_nsa_kernel, seq=seq, n_sel=n_sel, wlen=wlen, bounded=bounded, qblock=qblock),
        grid=(b, seq // qblock),
        in_specs=[qblk(ATTN_W), qblk(GATE_PAD)] + [per_b(nc)] * 3 + [per_b(seq)] * 6 + [full(ova), full(hot)],
        out_specs=qblk(ATTN_W),
        out_shape=jax.ShapeDtypeStruct((b, seq, ATTN_W), BF16),
        scratch_shapes=[] if bounded else [score_buf],
        compiler_params=pltpu.CompilerParams(dimension_semantics=("arbitrary", "arbitrary"),
                                             vmem_limit_bytes=VMEM_LIMIT),
        name="nsa_bounded" if bounded else "nsa",
    )(q, gates, kcmp, vcmp0, vcmp1, ksl, vsl0, vsl1, kwn, vwn0, vwn1, ova, hot)


_FF_CHUNKS = ((0, 1024), (1024, 1024), (2048, 768))


def _mix_ffn_kernel(x_ref, ya_ref, yb_ref, gm_ref, wa_ref, wb_ref, wo_ref, g2_ref, win_ref, cw_ref, cb_ref, wout_ref,
                    out_ref, gbuf, *, tiles_per_seq, tm):
    i = pl.program_id(0)

    @pl.when(i % tiles_per_seq == 0)
    def _():
        gbuf[0:SUBLANES, :] = jnp.zeros((SUBLANES, D_FF), F32)

    gm = gm_ref[...].astype(F32)
    mix = gm[:, 0:D_MODEL] * _dot(ya_ref[...], wa_ref[...]) + gm[:, D_MODEL:] * _dot(yb_ref[...], wb_ref[...])
    x = x_ref[...] + _dot(mix.astype(BF16), wo_ref[...])

    ms = jnp.mean(x * x, axis=-1, keepdims=True)
    h = (x * lax.rsqrt(ms + EPS) * g2_ref[...]).astype(BF16)
    acc = jnp.zeros((tm, D_MODEL), F32)
    for c0, w in _FF_CHUNKS:
        cols = slice(c0, c0 + w)
        gpre = _dot(h, win_ref[:, c0:c0 + w])
        up = _dot(h, win_ref[:, D_FF + c0:D_FF + c0 + w])
        gbuf[SUBLANES:SUBLANES + tm, cols] = gpre
        cw = cw_ref[:, cols]
        conv = (cw[0:1, :] * gbuf[SUBLANES - 2:SUBLANES - 2 + tm, cols]
                + cw[1:2, :] * gbuf[SUBLANES - 1:SUBLANES - 1 + tm, cols]
                + cw[2:3, :] * gpre) + cb_ref[:, cols]
        gbuf[0:SUBLANES, cols] = gbuf[tm:tm + SUBLANES, cols]
        act = conv * jax.nn.sigmoid(conv) * up
        acc = acc + _dot(act.astype(BF16), wout_ref[c0:c0 + w, :])
    out_ref[...] = x + acc


def _mix_ffn_call(x2, ya, yb, gm, wa, wb, wo, g2, win, conv_w, conv_b, wout, seq):
    n = x2.shape[0]
    tm = min(ROW_TILE, seq)
    row = lambda w: pl.BlockSpec((tm, w), lambda i: (i, 0))
    full = lambda a: pl.BlockSpec(a.shape, lambda i: (0,) * a.ndim)
    once = lambda a: pl.BlockSpec(a.shape, lambda i: (0,) * a.ndim, pipeline_mode=pl.Buffered(1))
    return pl.pallas_call(
        functools.partial(_mix_ffn_kernel, tiles_per_seq=seq // tm, tm=tm),
        grid=(n // tm,),
        in_specs=[row(D_MODEL), row(CONV_W), row(ATTN_W), row(2 * D_MODEL), once(wa), once(wb), once(wo), full(g2),
                  once(win), full(conv_w), full(conv_b), once(wout)],
        out_specs=row(D_MODEL),
        out_shape=jax.ShapeDtypeStruct((n, D_MODEL), F32),
        scratch_shapes=[pltpu.VMEM((tm + SUBLANES, D_FF), F32)],
        compiler_params=pltpu.CompilerParams(dimension_semantics=("arbitrary",), vmem_limit_bytes=VMEM_LIMIT),
        name="mix_ffn",
    )(x2, ya, yb, gm, wa, wb, wo, g2, win, conv_w, conv_b, wout)


def _block_diag2(w):
    z = jnp.zeros_like(w)
    return jnp.concatenate([jnp.concatenate([w, z], axis=-1), jnp.concatenate([z, w], axis=-1)], axis=-2)


def _static_tables(seq):
    nc, nb = seq // CMP_STRIDE, seq // SEL_BLOCK
    n_cmp = (seq - CMP_BLOCK) // CMP_STRIDE + 1
    ii = np.arange(nc)[None, :] * CMP_STRIDE
    jj = np.arange(nb)[:, None] * SEL_BLOCK
    ovt = ((ii < jj + SEL_BLOCK) & (ii + CMP_BLOCK > jj) & (np.arange(nc)[None, :] < n_cmp)).astype(np.float32)
    ova = np.ones((nc, LANES), np.float32)
    ova[:, :HEAD_DIM] = 0.0
    ova[:, :nb] = ovt.T
    hot = (np.arange(LANES)[None, :] == (np.arange(seq) // SEL_BLOCK)[:, None]).astype(np.float32)
    head = np.arange(SEG_W) // HEAD_DIM
    seg = (head[:, None] == head[None, :]).astype(np.float32)
    return jnp.asarray(ova, BF16), jnp.asarray(hot, BF16), jnp.asarray(seg, BF16)


def kernel(x, positions, norm1_g, w_in, conv_a_w, w_a_out, q_norm_g, k_norm_g, cmp_k_pe, cmp_k_w1, cmp_k_w2, cmp_v_pe, cmp_v_w1, cmp_v_w2, w_b_out, w_o, norm2_g, w_ffn_in, ffn_conv_w, ffn_conv_b, w_ffn_out):
    b, seq, d = x.shape
    n = b * seq
    half = HEAD_DIM // 2
    inv = ROPE_THETA ** (-jnp.arange(half, dtype=F32) / half)
    inv_lanes = jnp.tile(inv, LANES // half)[None, :]
    tm = min(ROW_TILE, seq)
    pos3 = positions.astype(F32).reshape(n // tm, tm // LANES, LANES)
    r_ang = jnp.arange(tm, dtype=F32)[:, None] * inv_lanes
    rcos, rsin = jnp.cos(r_ang), jnp.sin(r_ang)
    stepwise = jnp.all(pos3.reshape(n // tm, tm) == pos3[:, 0, 0:1] + jnp.arange(tm, dtype=F32)[None, :])
    ova, hot, seg = _static_tables(seq)
    gate_cols = N_BRANCH * N_HEADS

    x2 = x.reshape(n, d)
    for l in range(w_in.shape[0]):
        w = w_in[l]
        w_main = w[:, :_GATE_OFF].astype(BF16)
        w_gate = jnp.pad(w[:, _GATE_OFF:_GATE_OFF + gate_cols], ((0, 0), (0, GATE_PAD - gate_cols))).astype(BF16)
        w_gm = w[:, _GATE_OFF + gate_cols:].astype(BF16)
        qg = jnp.tile(q_norm_g[l], N_HEADS)[None, :]
        kg = jnp.tile(k_norm_g[l], (1, N_GROUPS))
        ya, q, kc, vc, ksl, vsl0, vsl1, kwn, vwn0, vwn1, gates, gm = lax.cond(
            stepwise, functools.partial(_proj_call, True, seq), functools.partial(_proj_call, False, seq),
            x2, pos3, norm1_g[l][None, :], w_main, w_gate, w_gm, conv_a_w[l], qg, kg[1:3].reshape(1, 2 * KV_W),
            inv_lanes, rcos, rsin, seg)

        per_b = lambda a: a.reshape(b, seq, a.shape[-1])
        w1_blocks = lambda w1: _block_diag2(w1.reshape(CMP_BLOCK, HEAD_DIM, HEAD_DIM)).astype(BF16)
        kcmp, vcmp0, vcmp1 = _compress_call(
            per_b(kc), per_b(vc), jnp.tile(cmp_k_pe[l], (1, N_GROUPS)), jnp.tile(cmp_v_pe[l], (1, N_GROUPS)),
            w1_blocks(cmp_k_w1[l]), _block_diag2(cmp_k_w2[l]).astype(BF16),
            w1_blocks(cmp_v_w1[l]), _block_diag2(cmp_v_w2[l]).astype(BF16), kg[0:1], seg)

        reach = HEAD_DIM * (HEAD_DIM ** -0.5 * LOG2E) * jnp.max(jnp.abs(q_norm_g[l])) * jnp.max(jnp.abs(k_norm_g[l]))
        nsa_args = (per_b(q), per_b(gates), kcmp, vcmp0, vcmp1, per_b(ksl), per_b(vsl0), per_b(vsl1), per_b(kwn),
                    per_b(vwn0), per_b(vwn1), ova, hot)
        yb = lax.cond(reach * 1.02 <= SCORE_BOUND, functools.partial(_nsa_call, True),
                      functools.partial(_nsa_call, False), *nsa_args)

        x2 = _mix_ffn_call(x2, ya, yb.reshape(n, ATTN_W), gm, w_a_out[l].astype(BF16), w_b_out[l].astype(BF16),
                           w_o[l].astype(BF16), norm2_g[l][None, :], w_ffn_in[l].astype(BF16), ffn_conv_w[l],
                           ffn_conv_b[l][None, :], w_ffn_out[l].astype(BF16), seq)
    return x2.reshape(b, seq, d)
```

```python
import functools

import jax
import jax.numpy as jnp
import numpy as np
from jax import lax
from jax.experimental import pallas as pl
from jax.experimental.pallas import tpu as pltpu

F32 = jnp.float32
BF16 = jnp.bfloat16

D_MODEL = 1024
CONV_W = 512
CONV_K = 3
N_HEADS = 8
HEAD_DIM = 64
N_GROUPS = 2
HEADS_PER_GROUP = N_HEADS // N_GROUPS
ATTN_W = N_HEADS * HEAD_DIM
KV_W = N_GROUPS * HEAD_DIM
CMP_BLOCK = 32
CMP_STRIDE = 16
SEL_BLOCK = 64
N_SEL = 16
WINDOW = 512
Q_BLOCK = 128
SWEEP_Q_BLOCK = 256
N_BRANCH = 3
ROPE_THETA = 10000.0
D_FF = 2816
EPS = 1e-6
NEG = -1e30
FORCE = 1e9
LOG2E = 1.4426950408889634
SCORE_BOUND = 60.0

LANES = 128
SUBLANES = 8
KEY_TILE = 512
SEG_W = 256
ROW_TILE = 512
GATE_PAD = LANES
VMEM_LIMIT = 56 * 1024 * 1024

_A_END = 3 * CONV_W
_Q_OFF = _A_END
_KV_OFF = _Q_OFF + ATTN_W
_GATE_OFF = _KV_OFF + 6 * KV_W


def _dot(a, b):
    return jnp.dot(a, b, preferred_element_type=F32)


def _dot_nt(a, b):
    return lax.dot_general(a, b, (((1,), (1,)), ((), ())), preferred_element_type=F32)


def _head_norm(x, gain, seg):
    sq = x * x
    hi = sq.astype(BF16)
    lo = (sq - hi.astype(F32)).astype(BF16)
    ms = (_dot(hi, seg) + _dot(lo, seg)) * (1.0 / HEAD_DIM)
    return x * lax.rsqrt(ms + EPS) * gain


def _rope(x, cos, sin_signed):
    lane = lax.broadcasted_iota(jnp.int32, x.shape, 1)
    half = HEAD_DIM // 2
    rot = jnp.where((lane & half) != 0, pltpu.roll(x, half, axis=1), pltpu.roll(x, LANES - half, axis=1))
    return x * cos + rot * sin_signed


def _proj_kernel(x_ref, pos_ref, g1_ref, w_ref, wgate_ref, wgm_ref, cw_ref, qg_ref, kg_ref, inv_ref, rcos_ref,
                 rsin_ref, seg_ref,
                 ya_ref, q_ref, kvc_ref, ksl_ref, vsl0_ref, vsl1_ref, kwn_ref, vwn0_ref, vwn1_ref, gate_ref,
                 gm_ref, ubuf, *, tiles_per_seq, tm, stepwise):
    i = pl.program_id(0)

    @pl.when(i % tiles_per_seq == 0)
    def _():
        ubuf[0:SUBLANES, :] = jnp.zeros((SUBLANES, CONV_W), F32)

    x = x_ref[...]
    ms = jnp.mean(x * x, axis=-1, keepdims=True)
    h = (x * lax.rsqrt(ms + EPS) * g1_ref[...]).astype(BF16)

    pa = _dot(h, w_ref[:, 0:_A_END])
    u = pa[:, CONV_W:2 * CONV_W] * pa[:, 2 * CONV_W:3 * CONV_W]
    ubuf[SUBLANES:SUBLANES + tm, :] = u
    cw = cw_ref[...]
    y = (cw[0:1, :] * ubuf[SUBLANES - 2:SUBLANES - 2 + tm, :]
         + cw[1:2, :] * ubuf[SUBLANES - 1:SUBLANES - 1 + tm, :]
         + cw[2:3, :] * u)
    ya_ref[...] = (pa[:, 0:CONV_W] * y).astype(BF16)
    ubuf[0:SUBLANES, :] = ubuf[tm:tm + SUBLANES, :]

    pos = pos_ref[0]
    inv = inv_ref[...]
    if stepwise:
        p0 = pos[0:1, 0:1]
        ca, sa = jnp.cos(p0 * inv), jnp.sin(p0 * inv)
        cb, sb = rcos_ref[...], rsin_ref[...]
        cos, sin = ca * cb - sa * sb, sa * cb + ca * sb
    else:
        cols = [jnp.broadcast_to(pos[c:c + 1, :], (LANES, LANES)).T for c in range(tm // LANES)]
        ang = jnp.concatenate(cols, axis=0) * inv
        cos, sin = jnp.cos(ang), jnp.sin(ang)
    lane = lax.broadcasted_iota(jnp.int32, cos.shape, 1)
    sin_signed = jnp.where((lane & (HEAD_DIM // 2)) != 0, sin, -sin)

    gm_ref[...] = jax.nn.sigmoid(_dot(h, wgm_ref[...])).astype(BF16)
    pb = _dot(h, w_ref[:, _Q_OFF:_GATE_OFF])
    scale = HEAD_DIM ** -0.5 * LOG2E
    seg = seg_ref[...]
    for c in range(ATTN_W // SEG_W):
        cols = slice(c * SEG_W, (c + 1) * SEG_W)
        qn = _head_norm(pb[:, cols], qg_ref[:, cols], seg)
        for t in range(SEG_W // LANES):
            lanes = slice(t * LANES, (t + 1) * LANES)
            q_ref[:, c * SEG_W + t * LANES:c * SEG_W + (t + 1) * LANES] = (
                _rope(qn[:, lanes], cos, sin_signed) * scale).astype(BF16)
    kv = pb[:, ATTN_W:ATTN_W + 6 * KV_W]
    kvc_ref[...] = kv[:, 0:2 * KV_W]
    kn = _head_norm(jnp.concatenate([kv[:, 2 * KV_W:3 * KV_W], kv[:, 4 * KV_W:5 * KV_W]], axis=1), kg_ref[...], seg)
    ksl_ref[...] = _rope(kn[:, 0:KV_W], cos, sin_signed).astype(BF16)
    kwn_ref[...] = _rope(kn[:, KV_W:2 * KV_W], cos, sin_signed).astype(BF16)
    first = lane < HEAD_DIM
    for v, out0, out1 in ((kv[:, 3 * KV_W:4 * KV_W], vsl0_ref, vsl1_ref), (kv[:, 5 * KV_W:6 * KV_W], vwn0_ref, vwn1_ref)):
        out0[...] = jnp.where(first, v, 1.0).astype(BF16)
        out1[...] = jnp.where(first, 1.0, v).astype(BF16)
    gate_ref[...] = jax.nn.sigmoid(_dot(h, wgate_ref[...]))


def _proj_call(stepwise, seq, x2, pos3, g1, w_main, w_gate, w_gm, conv_w, qg, kg, inv, rcos, rsin, seg):
    n = x2.shape[0]
    tm = min(ROW_TILE, seq)
    row = lambda w: pl.BlockSpec((tm, w), lambda i: (i, 0))
    full = lambda a: pl.BlockSpec(a.shape, lambda i: (0,) * a.ndim)
    once = lambda a: pl.BlockSpec(a.shape, lambda i: (0,) * a.ndim, pipeline_mode=pl.Buffered(1))
    out_shapes = (
        jax.ShapeDtypeStruct((n, CONV_W), BF16),
        jax.ShapeDtypeStruct((n, ATTN_W), BF16),
        jax.ShapeDtypeStruct((n, 2 * KV_W), F32),
        jax.ShapeDtypeStruct((n, KV_W), BF16),
        jax.ShapeDtypeStruct((n, KV_W), BF16),
        jax.ShapeDtypeStruct((n, KV_W), BF16),
        jax.ShapeDtypeStruct((n, KV_W), BF16),
        jax.ShapeDtypeStruct((n, KV_W), BF16),
        jax.ShapeDtypeStruct((n, KV_W), BF16),
        jax.ShapeDtypeStruct((n, GATE_PAD), F32),
        jax.ShapeDtypeStruct((n, 2 * D_MODEL), BF16),
    )
    return pl.pallas_call(
        functools.partial(_proj_kernel, tiles_per_seq=seq // tm, tm=tm, stepwise=stepwise),
        grid=(n // tm,),
        in_specs=[row(D_MODEL), pl.BlockSpec((1, tm // LANES, LANES), lambda i: (i, 0, 0)), full(g1), once(w_main),
                  once(w_gate), once(w_gm), full(conv_w), full(qg), full(kg), full(inv), full(rcos), full(rsin), full(seg)],
        out_specs=[row(s.shape[1]) for s in out_shapes],
        out_shape=out_shapes,
        scratch_shapes=[pltpu.VMEM((tm + SUBLANES, CONV_W), F32)],
        compiler_params=pltpu.CompilerParams(dimension_semantics=("arbitrary",), vmem_limit_bytes=VMEM_LIMIT),
        name="proj_stepwise" if stepwise else "proj",
    )(x2, pos3, g1, w_main, w_gate, w_gm, conv_w, qg, kg, inv, rcos, rsin, seg)


def _compress_kernel(kvc_ref, pe_ref, w1_ref, w2_ref, kg_ref, seg_ref, kcmp_ref, vcmp0_ref, vcmp1_ref, buf, *, seq):
    nc = seq // CMP_STRIDE
    for part in range(2):
        buf[part, 0:seq, :] = kvc_ref[0, :, part * KV_W:(part + 1) * KV_W]
        buf[part, seq:seq + CMP_BLOCK, :] = jnp.zeros((CMP_BLOCK, KV_W), F32)
    acc = jnp.zeros((nc, 2 * KV_W), F32)
    for r in range(CMP_BLOCK):
        xr = jnp.concatenate([buf[part, pl.ds(r, nc, stride=CMP_STRIDE), :] for part in range(2)], axis=1)
        acc = acc + _dot((xr + pe_ref[r:r + 1, :]).astype(BF16), w1_ref[r])
    hmid = acc * jax.nn.sigmoid(acc)
    o = _dot(hmid.astype(BF16), w2_ref[...])
    kcmp_ref[0] = _head_norm(o[:, 0:KV_W], kg_ref[...], seg_ref[0:KV_W, 0:KV_W]).astype(BF16)
    v = o[:, KV_W:2 * KV_W]
    first = lax.broadcasted_iota(jnp.int32, v.shape, 1) < HEAD_DIM
    vcmp0_ref[0] = jnp.where(first, v, 1.0).astype(BF16)
    vcmp1_ref[0] = jnp.where(first, 1.0, v).astype(BF16)


def _compress_call(kvc, pe, w1, w2, kg0, seg):
    b, seq, _ = kvc.shape
    nc = seq // CMP_STRIDE
    per_b = lambda rows, w: pl.BlockSpec((1, rows, w), lambda i: (i, 0, 0))
    full = lambda a: pl.BlockSpec(a.shape, lambda i: (0,) * a.ndim)
    return pl.pallas_call(
        functools.partial(_compress_kernel, seq=seq),
        grid=(b,),
        in_specs=[per_b(seq, 2 * KV_W), full(pe), full(w1), full(w2), full(kg0), full(seg)],
        out_specs=[per_b(nc, KV_W)] * 3,
        out_shape=(jax.ShapeDtypeStruct((b, nc, KV_W), BF16),) * 3,
        scratch_shapes=[pltpu.VMEM((2, seq + CMP_BLOCK, KV_W), F32)],
        compiler_params=pltpu.CompilerParams(dimension_semantics=("arbitrary",), vmem_limit_bytes=VMEM_LIMIT),
        name="compress",
    )(kvc, pe, w1, w2, kg0, seg)


def _loop_pairs(lo, hi, step, carry):
    pairs = (hi - lo) // 2
    carry = lax.fori_loop(0, pairs, lambda i, c: step((lo + 2 * i, lo + 2 * i + 1), c), carry)
    return lax.fori_loop(lo + 2 * pairs, hi, lambda kt, c: step((kt,), c), carry)


def _nsa_kernel(q_ref, gate_ref, kcmp_ref, vcmp0_ref, vcmp1_ref, ksel_ref, vsel0_ref, vsel1_ref, kwin_ref, vwin0_ref,
                vwin1_ref, ova_ref, hot_ref, yb_ref, *scratch, seq, n_sel, wlen, bounded, qblock):
    qb = pl.program_id(1)
    t0 = qb * qblock
    nc = seq // CMP_STRIDE
    nb = seq // SEL_BLOCK
    rows = HEADS_PER_GROUP * qblock
    groups = range(N_GROUPS)
    vcmp_refs = (vcmp0_ref, vcmp1_ref)
    vsel_refs = (vsel0_ref, vsel1_ref)
    vwin_refs = (vwin0_ref, vwin1_ref)

    q = q_ref[0]
    gates = gate_ref[0]
    lane_q = lax.broadcasted_iota(jnp.int32, (qblock, LANES), 1)
    t_col = t0 + lax.broadcasted_iota(jnp.int32, (qblock, 1), 0)
    rep = lambda a: jnp.concatenate([a] * HEADS_PER_GROUP, axis=0)

    def weights(s):
        if bounded:
            return jnp.exp2(s)
        m = jnp.maximum(jnp.max(s, axis=1, keepdims=True), 0.5 * NEG)
        return jnp.exp2(s - m)

    def normalized(acc):
        l = pltpu.roll(acc, HEAD_DIM, axis=1)
        return acc * jnp.where(l > 0.0, 1.0 / l, 0.0)

    def gated(o, g, branch):
        cols = [N_BRANCH * (g * HEADS_PER_GROUP + hl) + branch for hl in range(HEADS_PER_GROUP)]
        return jnp.concatenate([gates[:, c:c + 1] * o[hl * qblock:(hl + 1) * qblock] for hl, c in enumerate(cols)],
                               axis=0)

    cend = lax.broadcasted_iota(jnp.int32, (1, nc), 1) * CMP_STRIDE + (CMP_BLOCK - 1)
    done = rep(jnp.where(cend <= t_col, 0.0, NEG))

    qs, o_c, o_w, imp_t, lhs = [], [], [], [], []
    for g in groups:
        in_group = (lane_q >= HEAD_DIM * g) & (lane_q < HEAD_DIM * (g + 1))
        parts = []
        for hl in range(HEADS_PER_GROUP):
            pair, side = divmod(hl, 2)
            c0 = g * HEADS_PER_GROUP * HEAD_DIM + pair * LANES
            qp = q[:, c0:c0 + LANES].astype(F32)
            if side != g:
                qp = pltpu.roll(qp, HEAD_DIM, axis=1)
            parts.append(jnp.where(in_group, qp, 0.0).astype(BF16))
        qs.append(jnp.concatenate(parts, axis=0))

        p = weights(_dot_nt(qs[g], kcmp_ref[0]) + done)
        both = _dot(p.astype(BF16), jnp.concatenate([vcmp_refs[g][0], ova_ref[...]], axis=1))
        o_c.append(gated(normalized(both[:, 0:LANES]), g, 0))

        share = normalized(both[:, LANES:2 * LANES])
        imp = share[0:qblock]
        for hl in range(1, HEADS_PER_GROUP):
            imp = imp + share[hl * qblock:(hl + 1) * qblock]
        imp_t.append(imp.T[0:nb])

    def block_mask(imp_g, live):
        dead = [jnp.full((nb - live, qblock), NEG, F32)] if live < nb else []
        dead.append(jnp.zeros((LANES - nb, qblock), F32))
        if live <= n_sel:
            return jnp.concatenate([jnp.zeros((live, qblock), F32)] + dead, axis=0)
        blk = lax.broadcasted_iota(jnp.int32, (live, qblock), 0)
        cur = (t0 + lax.broadcasted_iota(jnp.int32, (live, qblock), 1)) // SEL_BLOCK
        forced = (blk == 0) | (blk == cur) | (blk == cur - 1)
        imp_a = jnp.where(forced, FORCE, jnp.where(blk <= cur, imp_g[0:live], -FORCE))
        tiles = [imp_a[r * SUBLANES:(r + 1) * SUBLANES] for r in range(live // SUBLANES)]
        sub = lax.broadcasted_iota(jnp.int32, (SUBLANES, qblock), 0)
        rank = [jnp.zeros((SUBLANES, qblock), jnp.int32) for _ in tiles]
        for jp in range(live):
            other = jnp.broadcast_to(imp_a[jp:jp + 1, :], (SUBLANES, qblock))
            for r, tile in enumerate(tiles):
                if r > jp // SUBLANES:
                    before = other >= tile
                elif r < jp // SUBLANES:
                    before = other > tile
                else:
                    before = (other > tile) | ((other == tile) & (sub > jp % SUBLANES))
                rank[r] = rank[r] + before.astype(jnp.int32)
        return jnp.concatenate([jnp.where(r < n_sel, 0.0, NEG) for r in rank] + dead, axis=0)

    live_steps = list(range(n_sel, nb + 1, n_sel))
    branches = [functools.partial(lambda live, *imps: tuple(block_mask(i, live) for i in imps), live)
                for live in live_steps]
    neg_t = lax.switch(((t0 + qblock - 1) // SEL_BLOCK) // n_sel, branches, *imp_t)
    for g in groups:
        lhs.append(jnp.concatenate([qs[g], rep(neg_t[g].T.astype(BF16))], axis=1))

    def scores(kt):
        k0 = pl.multiple_of(kt * KEY_TILE, KEY_TILE)
        rhs = jnp.concatenate([ksel_ref[0, pl.ds(k0, KEY_TILE), :], hot_ref[pl.ds(k0, KEY_TILE), :]], axis=1)
        kpos = k0 + lax.broadcasted_iota(jnp.int32, (1, KEY_TILE), 1)
        causal = rep(jnp.where(kpos <= t_col, 0.0, NEG))
        return [_dot_nt(lhs[g], rhs) + causal for g in groups]

    def values(kt, g):
        return vsel_refs[g][0, pl.ds(pl.multiple_of(kt * KEY_TILE, KEY_TILE), KEY_TILE), :]

    n_tiles = (t0 + qblock - 1) // KEY_TILE + 1
    zeros = tuple(jnp.zeros((rows, LANES), F32) for _ in groups)

    if bounded:
        def sweep(tiles, acc):
            s_t = [scores(kt) for kt in tiles]
            for kt, s_kt in zip(tiles, s_t):
                acc = tuple(acc[g] + _dot(jnp.exp2(s_kt[g]).astype(BF16), values(kt, g)) for g in groups)
            return acc

        acc_s = _loop_pairs(0, n_tiles, sweep, zeros)
    else:
        sbuf, = scratch

        lane_tiles = [slice(c * LANES, (c + 1) * LANES) for c in range(KEY_TILE // LANES)]

        def fold_max(m_run, s_t):
            return functools.reduce(jnp.maximum, [s_t[:, c] for c in lane_tiles], m_run)

        def pass1(tiles, m_run):
            for kt in tiles:
                s_t = scores(kt)
                for g in groups:
                    sbuf[g, kt] = s_t[g]
                m_run = tuple(fold_max(m_run[g], s_t[g]) for g in groups)
            return m_run

        m_run = _loop_pairs(0, n_tiles, pass1, tuple(jnp.full((rows, LANES), NEG, F32) for _ in groups))
        m_sel = [jnp.broadcast_to(jnp.max(m_run[g], axis=1, keepdims=True), (rows, LANES)) for g in groups]

        def pass2(tiles, acc):
            for kt in tiles:
                out = []
                for g in groups:
                    s_g = sbuf[g, kt]
                    p = jnp.concatenate([jnp.exp2(s_g[:, c] - m_sel[g]) for c in lane_tiles], axis=1).astype(BF16)
                    out.append(acc[g] + _dot(p, values(kt, g)))
                acc = tuple(out)
            return acc

        acc_s = _loop_pairs(0, n_tiles, pass2, zeros)

    start = pl.multiple_of(jnp.maximum(t0 + qblock - wlen, 0), qblock)
    diff = t_col - (start + lax.broadcasted_iota(jnp.int32, (1, wlen), 1))
    band = rep(jnp.where((diff >= 0) & (diff < WINDOW), 0.0, NEG))
    kw = kwin_ref[0, pl.ds(start, wlen), :]
    for g in groups:
        p = weights(_dot_nt(qs[g], kw) + band)
        o_w.append(gated(normalized(_dot(p.astype(BF16), vwin_refs[g][0, pl.ds(start, wlen), :])), g, 2))

    for g in groups:
        out = o_c[g] + gated(normalized(acc_s[g]), g, 1) + o_w[g]
        for pair in range(HEADS_PER_GROUP // 2):
            left = out[2 * pair * qblock:(2 * pair + 1) * qblock]
            right = out[(2 * pair + 1) * qblock:(2 * pair + 2) * qblock]
            if g == 0:
                right = pltpu.roll(right, HEAD_DIM, axis=1)
            else:
                left = pltpu.roll(left, HEAD_DIM, axis=1)
            c0 = g * HEADS_PER_GROUP * HEAD_DIM + pair * LANES
            yb_ref[0, :, c0:c0 + LANES] = jnp.where(lane_q < HEAD_DIM, left, right).astype(BF16)


def _nsa_call(bounded, q, gates, kcmp, vcmp0, vcmp1, ksl, vsl0, vsl1, kwn, vwn0, vwn1, ova, hot):
    b, seq, _ = q.shape
    nc = seq // CMP_STRIDE
    qblock = min(SWEEP_Q_BLOCK if bounded else Q_BLOCK, seq)
    wlen = min(WINDOW + qblock, seq)
    n_sel = min(N_SEL, seq // SEL_BLOCK)
    qblk = lambda w: pl.BlockSpec((1, qblock, w), lambda i, j: (i, j, 0))
    per_b = lambda rows: pl.BlockSpec((1, rows, KV_W), lambda i, j: (i, 0, 0))
    full = lambda a: pl.BlockSpec(a.shape, lambda i, j: (0,) * a.ndim)
    score_buf = pltpu.VMEM((N_GROUPS, seq // KEY_TILE, HEADS_PER_GROUP * qblock, KEY_TILE), F32)
    return pl.pallas_call(
        functools.partial(_nsa_kernel, seq=seq, n_sel=n_sel, wlen=wlen, bounded=bounded, qblock=qblock),
        grid=(b, seq // qblock),
        in_specs=[qblk(ATTN_W), qblk(GATE_PAD)] + [per_b(nc)] * 3 + [per_b(seq)] * 6 + [full(ova), full(hot)],
        out_specs=qblk(ATTN_W),
        out_shape=jax.ShapeDtypeStruct((b, seq, ATTN_W), BF16),
        scratch_shapes=[] if bounded else [score_buf],
        compiler_params=pltpu.CompilerParams(dimension_semantics=("arbitrary", "arbitrary"),
                                             vmem_limit_bytes=VMEM_LIMIT),
        name="nsa_bounded" if bounded else "nsa",
    )(q, gates, kcmp, vcmp0, vcmp1, ksl, vsl0, vsl1, kwn, vwn0, vwn1, ova, hot)


_FF_CHUNKS = ((0, 1024), (1024, 1024), (2048, 512), (2560, 256))


def _mix_ffn_kernel(x_ref, ya_ref, yb_ref, gm_ref, wa_ref, wb_ref, wo_ref, g2_ref, win_ref, cw_ref, cb_ref, wout_ref,
                    out_ref, gbuf, *, tiles_per_seq, tm):
    i = pl.program_id(0)

    @pl.when(i % tiles_per_seq == 0)
    def _():
        gbuf[0:SUBLANES, :] = jnp.zeros((SUBLANES, D_FF), F32)

    gm = gm_ref[...].astype(F32)
    mix = gm[:, 0:D_MODEL] * _dot(ya_ref[...], wa_ref[...]) + gm[:, D_MODEL:] * _dot(yb_ref[...], wb_ref[...])
    x = x_ref[...] + _dot(mix.astype(BF16), wo_ref[...])

    ms = jnp.mean(x * x, axis=-1, keepdims=True)
    h = (x * lax.rsqrt(ms + EPS) * g2_ref[...]).astype(BF16)
    def project(c0, w):
        return _dot(h, win_ref[:, c0:c0 + w]), _dot(h, win_ref[:, D_FF + c0:D_FF + c0 + w])

    acc = jnp.zeros((tm, D_MODEL), F32)
    ahead = project(*_FF_CHUNKS[0])
    for i, (c0, w) in enumerate(_FF_CHUNKS):
        cols = slice(c0, c0 + w)
        gpre, up = ahead
        if i + 1 < len(_FF_CHUNKS):
            ahead = project(*_FF_CHUNKS[i + 1])
        gbuf[SUBLANES:SUBLANES + tm, cols] = gpre
        cw = cw_ref[:, cols]
        conv = (cw[0:1, :] * gbuf[SUBLANES - 2:SUBLANES - 2 + tm, cols]
                + cw[1:2, :] * gbuf[SUBLANES - 1:SUBLANES - 1 + tm, cols]
                + cw[2:3, :] * gpre) + cb_ref[:, cols]
        gbuf[0:SUBLANES, cols] = gbuf[tm:tm + SUBLANES, cols]
        act = conv * jax.nn.sigmoid(conv) * up
        acc = acc + _dot(act.astype(BF16), wout_ref[c0:c0 + w, :])
    out_ref[...] = x + acc


def _mix_ffn_call(x2, ya, yb, gm, wa, wb, wo, g2, win, conv_w, conv_b, wout, seq):
    n = x2.shape[0]
    tm = min(ROW_TILE, seq)
    row = lambda w: pl.BlockSpec((tm, w), lambda i: (i, 0))
    full = lambda a: pl.BlockSpec(a.shape, lambda i: (0,) * a.ndim)
    once = lambda a: pl.BlockSpec(a.shape, lambda i: (0,) * a.ndim, pipeline_mode=pl.Buffered(1))
    return pl.pallas_call(
        functools.partial(_mix_ffn_kernel, tiles_per_seq=seq // tm, tm=tm),
        grid=(n // tm,),
        in_specs=[row(D_MODEL), row(CONV_W), row(ATTN_W), row(2 * D_MODEL), once(wa), once(wb), once(wo), full(g2),
                  once(win), full(conv_w), full(conv_b), once(wout)],
        out_specs=row(D_MODEL),
        out_shape=jax.ShapeDtypeStruct((n, D_MODEL), F32),
        scratch_shapes=[pltpu.VMEM((tm + SUBLANES, D_FF), F32)],
        compiler_params=pltpu.CompilerParams(dimension_semantics=("arbitrary",), vmem_limit_bytes=VMEM_LIMIT),
        name="mix_ffn",
    )(x2, ya, yb, gm, wa, wb, wo, g2, win, conv_w, conv_b, wout)


def _block_diag(*blocks):
    rows = []
    for i, blk in enumerate(blocks):
        rows.append(jnp.concatenate([blk if j == i else jnp.zeros_like(blk) for j in range(len(blocks))], axis=-1))
    return jnp.concatenate(rows, axis=-2)


def _static_tables(seq):
    nc, nb = seq // CMP_STRIDE, seq // SEL_BLOCK
    n_cmp = (seq - CMP_BLOCK) // CMP_STRIDE + 1
    ii = np.arange(nc)[None, :] * CMP_STRIDE
    jj = np.arange(nb)[:, None] * SEL_BLOCK
    ovt = ((ii < jj + SEL_BLOCK) & (ii + CMP_BLOCK > jj) & (np.arange(nc)[None, :] < n_cmp)).astype(np.float32)
    ova = np.ones((nc, LANES), np.float32)
    ova[:, :HEAD_DIM] = 0.0
    ova[:, :nb] = ovt.T
    hot = (np.arange(LANES)[None, :] == (np.arange(seq) // SEL_BLOCK)[:, None]).astype(np.float32)
    head = np.arange(SEG_W) // HEAD_DIM
    seg = (head[:, None] == head[None, :]).astype(np.float32)
    return jnp.asarray(ova, BF16), jnp.asarray(hot, BF16), jnp.asarray(seg, BF16)


def kernel(x, positions, norm1_g, w_in, conv_a_w, w_a_out, q_norm_g, k_norm_g, cmp_k_pe, cmp_k_w1, cmp_k_w2, cmp_v_pe, cmp_v_w1, cmp_v_w2, w_b_out, w_o, norm2_g, w_ffn_in, ffn_conv_w, ffn_conv_b, w_ffn_out):
    b, seq, d = x.shape
    n = b * seq
    half = HEAD_DIM // 2
    inv = ROPE_THETA ** (-jnp.arange(half, dtype=F32) / half)
    inv_lanes = jnp.tile(inv, LANES // half)[None, :]
    tm = min(ROW_TILE, seq)
    pos3 = positions.astype(F32).reshape(n // tm, tm // LANES, LANES)
    r_ang = jnp.arange(tm, dtype=F32)[:, None] * inv_lanes
    rcos, rsin = jnp.cos(r_ang), jnp.sin(r_ang)
    stepwise = jnp.all(pos3.reshape(n // tm, tm) == pos3[:, 0, 0:1] + jnp.arange(tm, dtype=F32)[None, :])
    ova, hot, seg = _static_tables(seq)
    gate_cols = N_BRANCH * N_HEADS

    x2 = x.reshape(n, d)
    for l in range(w_in.shape[0]):
        w = w_in[l]
        w_main = w[:, :_GATE_OFF].astype(BF16)
        w_gate = jnp.pad(w[:, _GATE_OFF:_GATE_OFF + gate_cols], ((0, 0), (0, GATE_PAD - gate_cols))).astype(BF16)
        w_gm = w[:, _GATE_OFF + gate_cols:].astype(BF16)
        qg = jnp.tile(q_norm_g[l], N_HEADS)[None, :]
        kg = jnp.tile(k_norm_g[l], (1, N_GROUPS))
        ya, q, kvc, ksl, vsl0, vsl1, kwn, vwn0, vwn1, gates, gm = lax.cond(
            stepwise, functools.partial(_proj_call, True, seq), functools.partial(_proj_call, False, seq),
            x2, pos3, norm1_g[l][None, :], w_main, w_gate, w_gm, conv_a_w[l], qg, kg[1:3].reshape(1, 2 * KV_W),
            inv_lanes, rcos, rsin, seg)

        per_b = lambda a: a.reshape(b, seq, a.shape[-1])
        w1k, w1v = (w.reshape(CMP_BLOCK, HEAD_DIM, HEAD_DIM) for w in (cmp_k_w1[l], cmp_v_w1[l]))
        kcmp, vcmp0, vcmp1 = _compress_call(
            per_b(kvc), jnp.concatenate([jnp.tile(cmp_k_pe[l], (1, N_GROUPS)), jnp.tile(cmp_v_pe[l], (1, N_GROUPS))], axis=1),
            _block_diag(w1k, w1k, w1v, w1v).astype(BF16),
            _block_diag(cmp_k_w2[l], cmp_k_w2[l], cmp_v_w2[l], cmp_v_w2[l]).astype(BF16), kg[0:1], seg)

        reach = HEAD_DIM * (HEAD_DIM ** -0.5 * LOG2E) * jnp.max(jnp.abs(q_norm_g[l])) * jnp.max(jnp.abs(k_norm_g[l]))
        nsa_args = (per_b(q), per_b(gates), kcmp, vcmp0, vcmp1, per_b(ksl), per_b(vsl0), per_b(vsl1), per_b(kwn),
                    per_b(vwn0), per_b(vwn1), ova, hot)
        yb = lax.cond(reach * 1.02 <= SCORE_BOUND, functools.partial(_nsa_call, True),
                      functools.partial(_nsa_call, False), *nsa_args)

        x2 = _mix_ffn_call(x2, ya, yb.reshape(n, ATTN_W), gm, w_a_out[l].astype(BF16), w_b_out[l].astype(BF16),
                           w_o[l].astype(BF16), norm2_g[l][None, :], w_ffn_in[l].astype(BF16), ffn_conv_w[l],
                           ffn_conv_b[l][None, :], w_ffn_out[l].astype(BF16), seq)
    return x2.reshape(b, seq, d)
```

```python
import functools

import jax
import jax.numpy as jnp
import numpy as np
from jax import lax
from jax.experimental import pallas as pl
from jax.experimental.pallas import tpu as pltpu

F32 = jnp.float32
BF16 = jnp.bfloat16

D_MODEL = 1024
CONV_W = 512
CONV_K = 3
N_HEADS = 8
HEAD_DIM = 64
N_GROUPS = 2
HEADS_PER_GROUP = N_HEADS // N_GROUPS
ATTN_W = N_HEADS * HEAD_DIM
KV_W = N_GROUPS * HEAD_DIM
CMP_BLOCK = 32
CMP_STRIDE = 16
SEL_BLOCK = 64
N_SEL = 16
WINDOW = 512
Q_BLOCK = 128
SWEEP_Q_BLOCK = 256
N_BRANCH = 3
ROPE_THETA = 10000.0
D_FF = 2816
EPS = 1e-6
NEG = -1e30
FORCE = 1e9
LOG2E = 1.4426950408889634
SCORE_BOUND = 60.0

LANES = 128
SUBLANES = 8
KEY_TILE = 512
SEG_W = 256
ROW_TILE = 512
GATE_PAD = LANES
V7X_VMEM_BYTES = 64 * 1024 * 1024
VMEM_LIMIT = V7X_VMEM_BYTES * 7 // 8

_A_END = 3 * CONV_W
_Q_OFF = _A_END
_KV_OFF = _Q_OFF + ATTN_W
_GATE_OFF = _KV_OFF + 6 * KV_W


def _dot(a, b):
    return jnp.dot(a, b, preferred_element_type=F32)


def _dot_nt(a, b):
    return lax.dot_general(a, b, (((1,), (1,)), ((), ())), preferred_element_type=F32)


def _head_norm(x, gain, seg):
    sq = x * x
    hi = sq.astype(BF16)
    lo = (sq - hi.astype(F32)).astype(BF16)
    ms = (_dot(hi, seg) + _dot(lo, seg)) * (1.0 / HEAD_DIM)
    return x * lax.rsqrt(ms + EPS) * gain


def _rope(x, cos, sin_signed):
    lane = lax.broadcasted_iota(jnp.int32, x.shape, 1)
    half = HEAD_DIM // 2
    rot = jnp.where((lane & half) != 0, pltpu.roll(x, half, axis=1), pltpu.roll(x, LANES - half, axis=1))
    return x * cos + rot * sin_signed


def _proj_kernel(x_ref, pos_ref, g1_ref, w_ref, wgate_ref, wgm_ref, cw_ref, qg_ref, kg_ref, inv_ref, rcos_ref,
                 rsin_ref, seg_ref,
                 ya_ref, q_ref, kvc_ref, ksl_ref, vsl0_ref, vsl1_ref, kwn_ref, vwn0_ref, vwn1_ref, gate_ref,
                 gm_ref, ubuf, *, tiles_per_seq, tm, stepwise):
    i = pl.program_id(0)

    @pl.when(i % tiles_per_seq == 0)
    def _():
        ubuf[0:SUBLANES, :] = jnp.zeros((SUBLANES, CONV_W), F32)

    x = x_ref[...]
    ms = jnp.mean(x * x, axis=-1, keepdims=True)
    h = (x * lax.rsqrt(ms + EPS) * g1_ref[...]).astype(BF16)

    pa = _dot(h, w_ref[:, 0:_A_END])
    u = pa[:, CONV_W:2 * CONV_W] * pa[:, 2 * CONV_W:3 * CONV_W]
    ubuf[SUBLANES:SUBLANES + tm, :] = u
    cw = cw_ref[...]
    y = (cw[0:1, :] * ubuf[SUBLANES - 2:SUBLANES - 2 + tm, :]
         + cw[1:2, :] * ubuf[SUBLANES - 1:SUBLANES - 1 + tm, :]
         + cw[2:3, :] * u)
    ya_ref[...] = (pa[:, 0:CONV_W] * y).astype(BF16)
    ubuf[0:SUBLANES, :] = ubuf[tm:tm + SUBLANES, :]

    pos = pos_ref[0]
    inv = inv_ref[...]
    if stepwise:
        p0 = pos[0:1, 0:1]
        ca, sa = jnp.cos(p0 * inv), jnp.sin(p0 * inv)
        cb, sb = rcos_ref[...], rsin_ref[...]
        cos, sin = ca * cb - sa * sb, sa * cb + ca * sb
    else:
        cols = [jnp.broadcast_to(pos[c:c + 1, :], (LANES, LANES)).T for c in range(tm // LANES)]
        ang = jnp.concatenate(cols, axis=0) * inv
        cos, sin = jnp.cos(ang), jnp.sin(ang)
    lane = lax.broadcasted_iota(jnp.int32, cos.shape, 1)
    sin_signed = jnp.where((lane & (HEAD_DIM // 2)) != 0, sin, -sin)

    gm_ref[...] = jax.nn.sigmoid(_dot(h, wgm_ref[...])).astype(BF16)
    pb = _dot(h, w_ref[:, _Q_OFF:_GATE_OFF])
    scale = HEAD_DIM ** -0.5 * LOG2E
    seg = seg_ref[...]
    for c in range(ATTN_W // SEG_W):
        cols = slice(c * SEG_W, (c + 1) * SEG_W)
        qn = _head_norm(pb[:, cols], qg_ref[:, cols], seg)
        for t in range(SEG_W // LANES):
            lanes = slice(t * LANES, (t + 1) * LANES)
            q_ref[:, c * SEG_W + t * LANES:c * SEG_W + (t + 1) * LANES] = (
                _rope(qn[:, lanes], cos, sin_signed) * scale).astype(BF16)
    kv = pb[:, ATTN_W:ATTN_W + 6 * KV_W]
    kvc_ref[...] = kv[:, 0:2 * KV_W]
    kn = _head_norm(jnp.concatenate([kv[:, 2 * KV_W:3 * KV_W], kv[:, 4 * KV_W:5 * KV_W]], axis=1), kg_ref[...], seg)
    ksl_ref[...] = _rope(kn[:, 0:KV_W], cos, sin_signed).astype(BF16)
    kwn_ref[...] = _rope(kn[:, KV_W:2 * KV_W], cos, sin_signed).astype(BF16)
    first = lane < HEAD_DIM
    for v, out0, out1 in ((kv[:, 3 * KV_W:4 * KV_W], vsl0_ref, vsl1_ref), (kv[:, 5 * KV_W:6 * KV_W], vwn0_ref, vwn1_ref)):
        out0[...] = jnp.where(first, v, 1.0).astype(BF16)
        out1[...] = jnp.where(first, 1.0, v).astype(BF16)
    gate_ref[...] = jax.nn.sigmoid(_dot(h, wgate_ref[...]))


def _proj_call(stepwise, seq, x2, pos3, g1, w_main, w_gate, w_gm, conv_w, qg, kg, inv, rcos, rsin, seg):
    n = x2.shape[0]
    tm = min(ROW_TILE, seq)
    row = lambda w: pl.BlockSpec((tm, w), lambda i: (i, 0))
    full = lambda a: pl.BlockSpec(a.shape, lambda i: (0,) * a.ndim)
    once = lambda a: pl.BlockSpec(a.shape, lambda i: (0,) * a.ndim, pipeline_mode=pl.Buffered(1))
    out_shapes = (
        jax.ShapeDtypeStruct((n, CONV_W), BF16),
        jax.ShapeDtypeStruct((n, ATTN_W), BF16),
        jax.ShapeDtypeStruct((n, 2 * KV_W), F32),
        jax.ShapeDtypeStruct((n, KV_W), BF16),
        jax.ShapeDtypeStruct((n, KV_W), BF16),
        jax.ShapeDtypeStruct((n, KV_W), BF16),
        jax.ShapeDtypeStruct((n, KV_W), BF16),
        jax.ShapeDtypeStruct((n, KV_W), BF16),
        jax.ShapeDtypeStruct((n, KV_W), BF16),
        jax.ShapeDtypeStruct((n, GATE_PAD), F32),
        jax.ShapeDtypeStruct((n, 2 * D_MODEL), BF16),
    )
    return pl.pallas_call(
        functools.partial(_proj_kernel, tiles_per_seq=seq // tm, tm=tm, stepwise=stepwise),
        grid=(n // tm,),
        in_specs=[row(D_MODEL), pl.BlockSpec((1, tm // LANES, LANES), lambda i: (i, 0, 0)), full(g1), once(w_main),
                  once(w_gate), once(w_gm), full(conv_w), full(qg), full(kg), full(inv), full(rcos), full(rsin), full(seg)],
        out_specs=[row(s.shape[1]) for s in out_shapes],
        out_shape=out_shapes,
        scratch_shapes=[pltpu.VMEM((tm + SUBLANES, CONV_W), F32)],
        compiler_params=pltpu.CompilerParams(dimension_semantics=("arbitrary",), vmem_limit_bytes=VMEM_LIMIT),
        name="proj_stepwise" if stepwise else "proj",
    )(x2, pos3, g1, w_main, w_gate, w_gm, conv_w, qg, kg, inv, rcos, rsin, seg)


def _compress_kernel(kvc_ref, pe_ref, w1_ref, w2_ref, kg_ref, seg_ref, kcmp_ref, vcmp0_ref, vcmp1_ref, buf, *, seq):
    nc = seq // CMP_STRIDE
    for part in range(2):
        buf[part, 0:seq, :] = kvc_ref[0, :, part * KV_W:(part + 1) * KV_W]
        buf[part, seq:seq + CMP_BLOCK, :] = jnp.zeros((CMP_BLOCK, KV_W), F32)
    acc = jnp.zeros((nc, 2 * KV_W), F32)
    for r in range(CMP_BLOCK):
        xr = jnp.concatenate([buf[part, pl.ds(r, nc, stride=CMP_STRIDE), :] for part in range(2)], axis=1)
        acc = acc + _dot((xr + pe_ref[r:r + 1, :]).astype(BF16), w1_ref[r])
    hmid = acc * jax.nn.sigmoid(acc)
    o = _dot(hmid.astype(BF16), w2_ref[...])
    kcmp_ref[0] = _head_norm(o[:, 0:KV_W], kg_ref[...], seg_ref[0:KV_W, 0:KV_W]).astype(BF16)
    v = o[:, KV_W:2 * KV_W]
    first = lax.broadcasted_iota(jnp.int32, v.shape, 1) < HEAD_DIM
    vcmp0_ref[0] = jnp.where(first, v, 1.0).astype(BF16)
    vcmp1_ref[0] = jnp.where(first, 1.0, v).astype(BF16)


def _compress_call(kvc, pe, w1, w2, kg0, seg):
    b, seq, _ = kvc.shape
    nc = seq // CMP_STRIDE
    per_b = lambda rows, w: pl.BlockSpec((1, rows, w), lambda i: (i, 0, 0))
    full = lambda a: pl.BlockSpec(a.shape, lambda i: (0,) * a.ndim)
    return pl.pallas_call(
        functools.partial(_compress_kernel, seq=seq),
        grid=(b,),
        in_specs=[per_b(seq, 2 * KV_W), full(pe), full(w1), full(w2), full(kg0), full(seg)],
        out_specs=[per_b(nc, KV_W)] * 3,
        out_shape=(jax.ShapeDtypeStruct((b, nc, KV_W), BF16),) * 3,
        scratch_shapes=[pltpu.VMEM((2, seq + CMP_BLOCK, KV_W), F32)],
        compiler_params=pltpu.CompilerParams(dimension_semantics=("arbitrary",), vmem_limit_bytes=VMEM_LIMIT),
        name="compress",
    )(kvc, pe, w1, w2, kg0, seg)


def _loop_pairs(lo, hi, step, carry):
    pairs = (hi - lo) // 2
    carry = lax.fori_loop(0, pairs, lambda i, c: step((lo + 2 * i, lo + 2 * i + 1), c), carry)
    return lax.fori_loop(lo + 2 * pairs, hi, lambda kt, c: step((kt,), c), carry)


def _nsa_kernel(q_ref, gate_ref, kcmp_ref, vcmp0_ref, vcmp1_ref, ksel_ref, vsel0_ref, vsel1_ref, kwin_ref, vwin0_ref,
                vwin1_ref, ova_ref, hot_ref, yb_ref, *scratch, seq, n_sel, wlen, bounded, qblock):
    qb = pl.program_id(1)
    t0 = qb * qblock
    nc = seq // CMP_STRIDE
    nb = seq // SEL_BLOCK
    rows = HEADS_PER_GROUP * qblock
    groups = range(N_GROUPS)
    vcmp_refs = (vcmp0_ref, vcmp1_ref)
    vsel_refs = (vsel0_ref, vsel1_ref)
    vwin_refs = (vwin0_ref, vwin1_ref)

    q = q_ref[0]
    gates = gate_ref[0]
    lane_q = lax.broadcasted_iota(jnp.int32, (qblock, LANES), 1)
    t_col = t0 + lax.broadcasted_iota(jnp.int32, (qblock, 1), 0)
    rep = lambda a: jnp.concatenate([a] * HEADS_PER_GROUP, axis=0)

    def weights(s):
        if bounded:
            return jnp.exp2(s)
        m = jnp.maximum(jnp.max(s, axis=1, keepdims=True), 0.5 * NEG)
        return jnp.exp2(s - m)

    def normalized(acc):
        l = pltpu.roll(acc, HEAD_DIM, axis=1)
        return acc * jnp.where(l > 0.0, 1.0 / l, 0.0)

    def gated(o, g, branch):
        cols = [N_BRANCH * (g * HEADS_PER_GROUP + hl) + branch for hl in range(HEADS_PER_GROUP)]
        return jnp.concatenate([gates[:, c:c + 1] * o[hl * qblock:(hl + 1) * qblock] for hl, c in enumerate(cols)],
                               axis=0)

    cend = lax.broadcasted_iota(jnp.int32, (1, nc), 1) * CMP_STRIDE + (CMP_BLOCK - 1)
    done = rep(jnp.where(cend <= t_col, 0.0, NEG))

    qs, o_c, o_w, imp_t, lhs = [], [], [], [], []
    for g in groups:
        in_group = (lane_q >= HEAD_DIM * g) & (lane_q < HEAD_DIM * (g + 1))
        parts = []
        for hl in range(HEADS_PER_GROUP):
            pair, side = divmod(hl, 2)
            c0 = g * HEADS_PER_GROUP * HEAD_DIM + pair * LANES
            qp = q[:, c0:c0 + LANES].astype(F32)
            if side != g:
                qp = pltpu.roll(qp, HEAD_DIM, axis=1)
            parts.append(jnp.where(in_group, qp, 0.0).astype(BF16))
        qs.append(jnp.concatenate(parts, axis=0))

        p = weights(_dot_nt(qs[g], kcmp_ref[0]) + done)
        both = _dot(p.astype(BF16), jnp.concatenate([vcmp_refs[g][0], ova_ref[...]], axis=1))
        o_c.append(gated(normalized(both[:, 0:LANES]), g, 0))

        share = normalized(both[:, LANES:2 * LANES])
        imp = share[0:qblock]
        for hl in range(1, HEADS_PER_GROUP):
            imp = imp + share[hl * qblock:(hl + 1) * qblock]
        imp_t.append(imp.T[0:nb])

    def block_mask(imp_g, live):
        dead = [jnp.full((nb - live, qblock), NEG, F32)] if live < nb else []
        dead.append(jnp.zeros((LANES - nb, qblock), F32))
        if live <= n_sel:
            return jnp.concatenate([jnp.zeros((live, qblock), F32)] + dead, axis=0)
        blk = lax.broadcasted_iota(jnp.int32, (live, qblock), 0)
        cur = (t0 + lax.broadcasted_iota(jnp.int32, (live, qblock), 1)) // SEL_BLOCK
        forced = (blk == 0) | (blk == cur) | (blk == cur - 1)
        imp_a = jnp.where(forced, FORCE, jnp.where(blk <= cur, imp_g[0:live], -FORCE))
        tiles = [imp_a[r * SUBLANES:(r + 1) * SUBLANES] for r in range(live // SUBLANES)]
        sub = lax.broadcasted_iota(jnp.int32, (SUBLANES, qblock), 0)
        rank = [jnp.zeros((SUBLANES, qblock), jnp.int32) for _ in tiles]
        for jp in range(live):
            other = jnp.broadcast_to(imp_a[jp:jp + 1, :], (SUBLANES, qblock))
            for r, tile in enumerate(tiles):
                if r > jp // SUBLANES:
                    before = other >= tile
                elif r < jp // SUBLANES:
                    before = other > tile
                else:
                    before = (other > tile) | ((other == tile) & (sub > jp % SUBLANES))
                rank[r] = rank[r] + before.astype(jnp.int32)
        return jnp.concatenate([jnp.where(r < n_sel, 0.0, NEG) for r in rank] + dead, axis=0)

    live_steps = list(range(n_sel, nb + 1, n_sel))
    branches = [functools.partial(lambda live, *imps: tuple(block_mask(i, live) for i in imps), live)
                for live in live_steps]
    neg_t = lax.switch(((t0 + qblock - 1) // SEL_BLOCK) // n_sel, branches, *imp_t)
    for g in groups:
        lhs.append(jnp.concatenate([qs[g], rep(neg_t[g].T.astype(BF16))], axis=1))

    def scores(kt):
        k0 = pl.multiple_of(kt * KEY_TILE, KEY_TILE)
        rhs = jnp.concatenate([ksel_ref[0, pl.ds(k0, KEY_TILE), :], hot_ref[pl.ds(k0, KEY_TILE), :]], axis=1)
        kpos = k0 + lax.broadcasted_iota(jnp.int32, (1, KEY_TILE), 1)
        causal = rep(jnp.where(kpos <= t_col, 0.0, NEG))
        return [_dot_nt(lhs[g], rhs) + causal for g in groups]

    def values(kt, g):
        return vsel_refs[g][0, pl.ds(pl.multiple_of(kt * KEY_TILE, KEY_TILE), KEY_TILE), :]

    n_tiles = (t0 + qblock - 1) // KEY_TILE + 1
    zeros = tuple(jnp.zeros((rows, LANES), F32) for _ in groups)

    if bounded:
        def sweep(tiles, acc):
            s_t = [scores(kt) for kt in tiles]
            for kt, s_kt in zip(tiles, s_t):
                acc = tuple(acc[g] + _dot(jnp.exp2(s_kt[g]).astype(BF16), values(kt, g)) for g in groups)
            return acc

        acc_s = _loop_pairs(0, n_tiles, sweep, zeros)
    else:
        sbuf, = scratch

        lane_tiles = [slice(c * LANES, (c + 1) * LANES) for c in range(KEY_TILE // LANES)]

        def fold_max(m_run, s_t):
            return functools.reduce(jnp.maximum, [s_t[:, c] for c in lane_tiles], m_run)

        def pass1(tiles, m_run):
            for kt in tiles:
                s_t = scores(kt)
                for g in groups:
                    sbuf[g, kt] = s_t[g]
                m_run = tuple(fold_max(m_run[g], s_t[g]) for g in groups)
            return m_run

        m_run = _loop_pairs(0, n_tiles, pass1, tuple(jnp.full((rows, LANES), NEG, F32) for _ in groups))
        m_sel = [jnp.broadcast_to(jnp.max(m_run[g], axis=1, keepdims=True), (rows, LANES)) for g in groups]

        def pass2(tiles, acc):
            for kt in tiles:
                out = []
                for g in groups:
                    s_g = sbuf[g, kt]
                    p = jnp.concatenate([jnp.exp2(s_g[:, c] - m_sel[g]) for c in lane_tiles], axis=1).astype(BF16)
                    out.append(acc[g] + _dot(p, values(kt, g)))
                acc = tuple(out)
            return acc

        acc_s = _loop_pairs(0, n_tiles, pass2, zeros)

    start = pl.multiple_of(jnp.maximum(t0 + qblock - wlen, 0), qblock)
    diff = t_col - (start + lax.broadcasted_iota(jnp.int32, (1, wlen), 1))
    band = rep(jnp.where((diff >= 0) & (diff < WINDOW), 0.0, NEG))
    kw = kwin_ref[0, pl.ds(start, wlen), :]
    for g in groups:
        p = weights(_dot_nt(qs[g], kw) + band)
        o_w.append(gated(normalized(_dot(p.astype(BF16), vwin_refs[g][0, pl.ds(start, wlen), :])), g, 2))

    for g in groups:
        out = o_c[g] + gated(normalized(acc_s[g]), g, 1) + o_w[g]
        for pair in range(HEADS_PER_GROUP // 2):
            left = out[2 * pair * qblock:(2 * pair + 1) * qblock]
            right = out[(2 * pair + 1) * qblock:(2 * pair + 2) * qblock]
            if g == 0:
                right = pltpu.roll(right, HEAD_DIM, axis=1)
            else:
                left = pltpu.roll(left, HEAD_DIM, axis=1)
            c0 = g * HEADS_PER_GROUP * HEAD_DIM + pair * LANES
            yb_ref[0, :, c0:c0 + LANES] = jnp.where(lane_q < HEAD_DIM, left, right).astype(BF16)


def _nsa_call(bounded, q, gates, kcmp, vcmp0, vcmp1, ksl, vsl0, vsl1, kwn, vwn0, vwn1, ova, hot):
    b, seq, _ = q.shape
    nc = seq // CMP_STRIDE
    qblock = min(SWEEP_Q_BLOCK if bounded else Q_BLOCK, seq)
    wlen = min(WINDOW + qblock, seq)
    n_sel = min(N_SEL, seq // SEL_BLOCK)
    qblk = lambda w: pl.BlockSpec((1, qblock, w), lambda i, j: (i, j, 0))
    per_b = lambda rows: pl.BlockSpec((1, rows, KV_W), lambda i, j: (i, 0, 0))
    full = lambda a: pl.BlockSpec(a.shape, lambda i, j: (0,) * a.ndim)
    score_buf = pltpu.VMEM((N_GROUPS, seq // KEY_TILE, HEADS_PER_GROUP * qblock, KEY_TILE), F32)
    return pl.pallas_call(
        functools.partial(_nsa_kernel, seq=seq, n_sel=n_sel, wlen=wlen, bounded=bounded, qblock=qblock),
        grid=(b, seq // qblock),
        in_specs=[qblk(ATTN_W), qblk(GATE_PAD)] + [per_b(nc)] * 3 + [per_b(seq)] * 6 + [full(ova), full(hot)],
        out_specs=qblk(ATTN_W),
        out_shape=jax.ShapeDtypeStruct((b, seq, ATTN_W), BF16),
        scratch_shapes=[] if bounded else [score_buf],
        compiler_params=pltpu.CompilerParams(dimension_semantics=("arbitrary", "arbitrary"),
                                             vmem_limit_bytes=VMEM_LIMIT),
        name="nsa_bounded" if bounded else "nsa",
    )(q, gates, kcmp, vcmp0, vcmp1, ksl, vsl0, vsl1, kwn, vwn0, vwn1, ova, hot)


_FF_CHUNKS = ((0, 1024), (1024, 1024), (2048, 512), (2560, 256))


def _mix_ffn_kernel(x_ref, ya_ref, yb_ref, gm_ref, wa_ref, wb_ref, wo_ref, g2_ref, win_ref, cw_ref, cb_ref, wout_ref,
                    out_ref, gbuf, *, tiles_per_seq, tm):
    i = pl.program_id(0)

    @pl.when(i % tiles_per_seq == 0)
    def _():
        gbuf[0:SUBLANES, :] = jnp.zeros((SUBLANES, D_FF), F32)

    gm = gm_ref[...].astype(F32)
    mix = gm[:, 0:D_MODEL] * _dot(ya_ref[...], wa_ref[...]) + gm[:, D_MODEL:] * _dot(yb_ref[...], wb_ref[...])
    x = x_ref[...] + _dot(mix.astype(BF16), wo_ref[...])

    ms = jnp.mean(x * x, axis=-1, keepdims=True)
    h = (x * lax.rsqrt(ms + EPS) * g2_ref[...]).astype(BF16)
    def project(c0, w):
        return _dot(h, win_ref[:, c0:c0 + w]), _dot(h, win_ref[:, D_FF + c0:D_FF + c0 + w])

    acc = jnp.zeros((tm, D_MODEL), F32)
    ahead = project(*_FF_CHUNKS[0])
    for i, (c0, w) in enumerate(_FF_CHUNKS):
        cols = slice(c0, c0 + w)
        gpre, up = ahead
        if i + 1 < len(_FF_CHUNKS):
            ahead = project(*_FF_CHUNKS[i + 1])
        gbuf[SUBLANES:SUBLANES + tm, cols] = gpre
        cw = cw_ref[:, cols]
        conv = (cw[0:1, :] * gbuf[SUBLANES - 2:SUBLANES - 2 + tm, cols]
                + cw[1:2, :] * gbuf[SUBLANES - 1:SUBLANES - 1 + tm, cols]
                + cw[2:3, :] * gpre) + cb_ref[:, cols]
        gbuf[0:SUBLANES, cols] = gbuf[tm:tm + SUBLANES, cols]
        act = conv * jax.nn.sigmoid(conv) * up
        acc = acc + _dot(act.astype(BF16), wout_ref[c0:c0 + w, :])
    out_ref[...] = x + acc


def _mix_ffn_call(x2, ya, yb, gm, wa, wb, wo, g2, win, conv_w, conv_b, wout, seq):
    n = x2.shape[0]
    tm = min(ROW_TILE, seq)
    row = lambda w: pl.BlockSpec((tm, w), lambda i: (i, 0))
    full = lambda a: pl.BlockSpec(a.shape, lambda i: (0,) * a.ndim)
    once = lambda a: pl.BlockSpec(a.shape, lambda i: (0,) * a.ndim, pipeline_mode=pl.Buffered(1))
    return pl.pallas_call(
        functools.partial(_mix_ffn_kernel, tiles_per_seq=seq // tm, tm=tm),
        grid=(n // tm,),
        in_specs=[row(D_MODEL), row(CONV_W), row(ATTN_W), row(2 * D_MODEL), once(wa), once(wb), once(wo), full(g2),
                  once(win), full(conv_w), full(conv_b), once(wout)],
        out_specs=row(D_MODEL),
        out_shape=jax.ShapeDtypeStruct((n, D_MODEL), F32),
        scratch_shapes=[pltpu.VMEM((tm + SUBLANES, D_FF), F32)],
        compiler_params=pltpu.CompilerParams(dimension_semantics=("arbitrary",), vmem_limit_bytes=VMEM_LIMIT),
        name="mix_ffn",
    )(x2, ya, yb, gm, wa, wb, wo, g2, win, conv_w, conv_b, wout)


def _block_diag(*blocks):
    rows = []
    for i, blk in enumerate(blocks):
        rows.append(jnp.concatenate([blk if j == i else jnp.zeros_like(blk) for j in range(len(blocks))], axis=-1))
    return jnp.concatenate(rows, axis=-2)


def _static_tables(seq):
    nc, nb = seq // CMP_STRIDE, seq // SEL_BLOCK
    n_cmp = (seq - CMP_BLOCK) // CMP_STRIDE + 1
    ii = np.arange(nc)[None, :] * CMP_STRIDE
    jj = np.arange(nb)[:, None] * SEL_BLOCK
    ovt = ((ii < jj + SEL_BLOCK) & (ii + CMP_BLOCK > jj) & (np.arange(nc)[None, :] < n_cmp)).astype(np.float32)
    ova = np.ones((nc, LANES), np.float32)
    ova[:, :HEAD_DIM] = 0.0
    ova[:, :nb] = ovt.T
    hot = (np.arange(LANES)[None, :] == (np.arange(seq) // SEL_BLOCK)[:, None]).astype(np.float32)
    head = np.arange(SEG_W) // HEAD_DIM
    seg = (head[:, None] == head[None, :]).astype(np.float32)
    return jnp.asarray(ova, BF16), jnp.asarray(hot, BF16), jnp.asarray(seg, BF16)


def kernel(x, positions, norm1_g, w_in, conv_a_w, w_a_out, q_norm_g, k_norm_g, cmp_k_pe, cmp_k_w1, cmp_k_w2, cmp_v_pe, cmp_v_w1, cmp_v_w2, w_b_out, w_o, norm2_g, w_ffn_in, ffn_conv_w, ffn_conv_b, w_ffn_out):
    b, seq, d = x.shape
    n = b * seq
    half = HEAD_DIM // 2
    inv = ROPE_THETA ** (-jnp.arange(half, dtype=F32) / half)
    inv_lanes = jnp.tile(inv, LANES // half)[None, :]
    tm = min(ROW_TILE, seq)
    pos3 = positions.astype(F32).reshape(n // tm, tm // LANES, LANES)
    r_ang = jnp.arange(tm, dtype=F32)[:, None] * inv_lanes
    rcos, rsin = jnp.cos(r_ang), jnp.sin(r_ang)
    stepwise = jnp.all(pos3.reshape(n // tm, tm) == pos3[:, 0, 0:1] + jnp.arange(tm, dtype=F32)[None, :])
    ova, hot, seg = _static_tables(seq)
    gate_cols = N_BRANCH * N_HEADS

    x2 = x.reshape(n, d)
    for l in range(w_in.shape[0]):
        w = w_in[l]
        w_main = w[:, :_GATE_OFF].astype(BF16)
        w_gate = jnp.pad(w[:, _GATE_OFF:_GATE_OFF + gate_cols], ((0, 0), (0, GATE_PAD - gate_cols))).astype(BF16)
        w_gm = w[:, _GATE_OFF + gate_cols:].astype(BF16)
        qg = jnp.tile(q_norm_g[l], N_HEADS)[None, :]
        kg = jnp.tile(k_norm_g[l], (1, N_GROUPS))
        ya, q, kvc, ksl, vsl0, vsl1, kwn, vwn0, vwn1, gates, gm = lax.cond(
            stepwise, functools.partial(_proj_call, True, seq), functools.partial(_proj_call, False, seq),
            x2, pos3, norm1_g[l][None, :], w_main, w_gate, w_gm, conv_a_w[l], qg, kg[1:3].reshape(1, 2 * KV_W),
            inv_lanes, rcos, rsin, seg)

        per_b = lambda a: a.reshape(b, seq, a.shape[-1])
        w1k, w1v = (w.reshape(CMP_BLOCK, HEAD_DIM, HEAD_DIM) for w in (cmp_k_w1[l], cmp_v_w1[l]))
        kcmp, vcmp0, vcmp1 = _compress_call(
            per_b(kvc), jnp.concatenate([jnp.tile(cmp_k_pe[l], (1, N_GROUPS)), jnp.tile(cmp_v_pe[l], (1, N_GROUPS))], axis=1),
            _block_diag(w1k, w1k, w1v, w1v).astype(BF16),
            _block_diag(cmp_k_w2[l], cmp_k_w2[l], cmp_v_w2[l], cmp_v_w2[l]).astype(BF16), kg[0:1], seg)

        reach = HEAD_DIM * (HEAD_DIM ** -0.5 * LOG2E) * jnp.max(jnp.abs(q_norm_g[l])) * jnp.max(jnp.abs(k_norm_g[l]))
        nsa_args = (per_b(q), per_b(gates), kcmp, vcmp0, vcmp1, per_b(ksl), per_b(vsl0), per_b(vsl1), per_b(kwn),
                    per_b(vwn0), per_b(vwn1), ova, hot)
        yb = lax.cond(reach * 1.02 <= SCORE_BOUND, functools.partial(_nsa_call, True),
                      functools.partial(_nsa_call, False), *nsa_args)

        x2 = _mix_ffn_call(x2, ya, yb.reshape(n, ATTN_W), gm, w_a_out[l].astype(BF16), w_b_out[l].astype(BF16),
                           w_o[l].astype(BF16), norm2_g[l][None, :], w_ffn_in[l].astype(BF16), ffn_conv_w[l],
                           ffn_conv_b[l][None, :], w_ffn_out[l].astype(BF16), seq)
    return x2.reshape(b, seq, d)
```

```python
import functools

import jax
import jax.numpy as jnp
import numpy as np
from jax import lax
from jax.experimental import pallas as pl
from jax.experimental.pallas import tpu as pltpu

F32 = jnp.float32
BF16 = jnp.bfloat16

D_MODEL = 1024
CONV_W = 512
CONV_K = 3
N_HEADS = 8
HEAD_DIM = 64
N_GROUPS = 2
HEADS_PER_GROUP = N_HEADS // N_GROUPS
ATTN_W = N_HEADS * HEAD_DIM
KV_W = N_GROUPS * HEAD_DIM
CMP_BLOCK = 32
CMP_STRIDE = 16
SEL_BLOCK = 64
N_SEL = 16
WINDOW = 512
Q_BLOCK = 128
SWEEP_Q_BLOCK = 256
N_BRANCH = 3
ROPE_THETA = 10000.0
D_FF = 2816
EPS = 1e-6
NEG = -1e30
FORCE = 1e9
LOG2E = 1.4426950408889634
SCORE_BOUND = 60.0

LANES = 128
SUBLANES = 8
KEY_TILE = 512
SEG_W = 256
ROW_TILE = 512
GATE_PAD = LANES
V7X_VMEM_BYTES = 64 * 1024 * 1024
VMEM_LIMIT = V7X_VMEM_BYTES * 7 // 8

_A_END = 3 * CONV_W
_Q_OFF = _A_END
_KV_OFF = _Q_OFF + ATTN_W
_GATE_OFF = _KV_OFF + 6 * KV_W


def _dot(a, b):
    return jnp.dot(a, b, preferred_element_type=F32)


def _dot_nt(a, b):
    return lax.dot_general(a, b, (((1,), (1,)), ((), ())), preferred_element_type=F32)


def _head_norm(x, gain, seg):
    sq = x * x
    hi = sq.astype(BF16)
    lo = (sq - hi.astype(F32)).astype(BF16)
    ms = (_dot(hi, seg) + _dot(lo, seg)) * (1.0 / HEAD_DIM)
    return x * lax.rsqrt(ms + EPS) * gain


def _rope(x, cos, sin_signed):
    lane = lax.broadcasted_iota(jnp.int32, x.shape, 1)
    half = HEAD_DIM // 2
    rot = jnp.where((lane & half) != 0, pltpu.roll(x, half, axis=1), pltpu.roll(x, LANES - half, axis=1))
    return x * cos + rot * sin_signed


def _proj_kernel(x_ref, pos_ref, g1_ref, w_ref, wgate_ref, wgm_ref, cw_ref, qg_ref, kg_ref, inv_ref, rcos_ref,
                 rsin_ref, seg_ref,
                 ya_ref, q_ref, kvc_ref, ksl_ref, vsl0_ref, vsl1_ref, kwn_ref, vwn0_ref, vwn1_ref, gate_ref,
                 gm_ref, ubuf, *, tiles_per_seq, tm, stepwise):
    i = pl.program_id(0)

    @pl.when(i % tiles_per_seq == 0)
    def _():
        ubuf[0:SUBLANES, :] = jnp.zeros((SUBLANES, CONV_W), F32)

    x = x_ref[...]
    ms = jnp.mean(x * x, axis=-1, keepdims=True)
    h = (x * lax.rsqrt(ms + EPS) * g1_ref[...]).astype(BF16)

    pa = _dot(h, w_ref[:, 0:_A_END])
    u = pa[:, CONV_W:2 * CONV_W] * pa[:, 2 * CONV_W:3 * CONV_W]
    ubuf[SUBLANES:SUBLANES + tm, :] = u
    cw = cw_ref[...]
    y = (cw[0:1, :] * ubuf[SUBLANES - 2:SUBLANES - 2 + tm, :]
         + cw[1:2, :] * ubuf[SUBLANES - 1:SUBLANES - 1 + tm, :]
         + cw[2:3, :] * u)
    ya_ref[...] = (pa[:, 0:CONV_W] * y).astype(BF16)
    ubuf[0:SUBLANES, :] = ubuf[tm:tm + SUBLANES, :]

    pos = pos_ref[0]
    inv = inv_ref[...]
    if stepwise:
        p0 = pos[0:1, 0:1]
        ca, sa = jnp.cos(p0 * inv), jnp.sin(p0 * inv)
        cb, sb = rcos_ref[...], rsin_ref[...]
        cos, sin = ca * cb - sa * sb, sa * cb + ca * sb
    else:
        cols = [jnp.broadcast_to(pos[c:c + 1, :], (LANES, LANES)).T for c in range(tm // LANES)]
        ang = jnp.concatenate(cols, axis=0) * inv
        cos, sin = jnp.cos(ang), jnp.sin(ang)
    lane = lax.broadcasted_iota(jnp.int32, cos.shape, 1)
    sin_signed = jnp.where((lane & (HEAD_DIM // 2)) != 0, sin, -sin)

    gm_ref[...] = jax.nn.sigmoid(_dot(h, wgm_ref[...])).astype(BF16)
    pb = _dot(h, w_ref[:, _Q_OFF:_GATE_OFF])
    scale = HEAD_DIM ** -0.5 * LOG2E
    seg = seg_ref[...]
    for c in range(ATTN_W // SEG_W):
        cols = slice(c * SEG_W, (c + 1) * SEG_W)
        qn = _head_norm(pb[:, cols], qg_ref[:, cols], seg)
        for t in range(SEG_W // LANES):
            lanes = slice(t * LANES, (t + 1) * LANES)
            q_ref[:, c * SEG_W + t * LANES:c * SEG_W + (t + 1) * LANES] = (
                _rope(qn[:, lanes], cos, sin_signed) * scale).astype(BF16)
    kv = pb[:, ATTN_W:ATTN_W + 6 * KV_W]
    kvc_ref[...] = kv[:, 0:2 * KV_W]
    kn = _head_norm(jnp.concatenate([kv[:, 2 * KV_W:3 * KV_W], kv[:, 4 * KV_W:5 * KV_W]], axis=1), kg_ref[...], seg)
    ksl_ref[...] = _rope(kn[:, 0:KV_W], cos, sin_signed).astype(BF16)
    kwn_ref[...] = _rope(kn[:, KV_W:2 * KV_W], cos, sin_signed).astype(BF16)
    first = lane < HEAD_DIM
    for v, out0, out1 in ((kv[:, 3 * KV_W:4 * KV_W], vsl0_ref, vsl1_ref), (kv[:, 5 * KV_W:6 * KV_W], vwn0_ref, vwn1_ref)):
        out0[...] = jnp.where(first, v, 1.0).astype(BF16)
        out1[...] = jnp.where(first, 1.0, v).astype(BF16)
    gate_ref[...] = jax.nn.sigmoid(_dot(h, wgate_ref[...]))


def _proj_call(stepwise, seq, x2, pos3, g1, w_main, w_gate, w_gm, conv_w, qg, kg, inv, rcos, rsin, seg):
    n = x2.shape[0]
    tm = min(ROW_TILE, seq)
    row = lambda w: pl.BlockSpec((tm, w), lambda i: (i, 0))
    full = lambda a: pl.BlockSpec(a.shape, lambda i: (0,) * a.ndim)
    once = lambda a: pl.BlockSpec(a.shape, lambda i: (0,) * a.ndim, pipeline_mode=pl.Buffered(1))
    out_shapes = (
        jax.ShapeDtypeStruct((n, CONV_W), BF16),
        jax.ShapeDtypeStruct((n, ATTN_W), BF16),
        jax.ShapeDtypeStruct((n, 2 * KV_W), F32),
        jax.ShapeDtypeStruct((n, KV_W), BF16),
        jax.ShapeDtypeStruct((n, KV_W), BF16),
        jax.ShapeDtypeStruct((n, KV_W), BF16),
        jax.ShapeDtypeStruct((n, KV_W), BF16),
        jax.ShapeDtypeStruct((n, KV_W), BF16),
        jax.ShapeDtypeStruct((n, KV_W), BF16),
        jax.ShapeDtypeStruct((n, GATE_PAD), F32),
        jax.ShapeDtypeStruct((n, 2 * D_MODEL), BF16),
    )
    return pl.pallas_call(
        functools.partial(_proj_kernel, tiles_per_seq=seq // tm, tm=tm, stepwise=stepwise),
        grid=(n // tm,),
        in_specs=[row(D_MODEL), pl.BlockSpec((1, tm // LANES, LANES), lambda i: (i, 0, 0)), full(g1), once(w_main),
                  once(w_gate), once(w_gm), full(conv_w), full(qg), full(kg), full(inv), full(rcos), full(rsin), full(seg)],
        out_specs=[row(s.shape[1]) for s in out_shapes],
        out_shape=out_shapes,
        scratch_shapes=[pltpu.VMEM((tm + SUBLANES, CONV_W), F32)],
        compiler_params=pltpu.CompilerParams(dimension_semantics=("arbitrary",), vmem_limit_bytes=VMEM_LIMIT),
        name="proj_stepwise" if stepwise else "proj",
    )(x2, pos3, g1, w_main, w_gate, w_gm, conv_w, qg, kg, inv, rcos, rsin, seg)


def _compress_kernel(kvc_ref, pe_ref, w1_ref, w2_ref, kg_ref, seg_ref, kcmp_ref, vcmp0_ref, vcmp1_ref, buf, *, seq):
    nc = seq // CMP_STRIDE
    for part in range(2):
        buf[part, 0:seq, :] = kvc_ref[0, :, part * KV_W:(part + 1) * KV_W]
        buf[part, seq:seq + CMP_BLOCK, :] = jnp.zeros((CMP_BLOCK, KV_W), F32)
    acc = jnp.zeros((nc, 2 * KV_W), F32)
    for r in range(CMP_BLOCK):
        xr = jnp.concatenate([buf[part, pl.ds(r, nc, stride=CMP_STRIDE), :] for part in range(2)], axis=1)
        acc = acc + _dot((xr + pe_ref[r:r + 1, :]).astype(BF16), w1_ref[r])
    hmid = acc * jax.nn.sigmoid(acc)
    o = _dot(hmid.astype(BF16), w2_ref[...])
    kcmp_ref[0] = _head_norm(o[:, 0:KV_W], kg_ref[...], seg_ref[0:KV_W, 0:KV_W]).astype(BF16)
    v = o[:, KV_W:2 * KV_W]
    first = lax.broadcasted_iota(jnp.int32, v.shape, 1) < HEAD_DIM
    vcmp0_ref[0] = jnp.where(first, v, 1.0).astype(BF16)
    vcmp1_ref[0] = jnp.where(first, 1.0, v).astype(BF16)


def _compress_call(kvc, pe, w1, w2, kg0, seg):
    b, seq, _ = kvc.shape
    nc = seq // CMP_STRIDE
    per_b = lambda rows, w: pl.BlockSpec((1, rows, w), lambda i: (i, 0, 0))
    full = lambda a: pl.BlockSpec(a.shape, lambda i: (0,) * a.ndim)
    return pl.pallas_call(
        functools.partial(_compress_kernel, seq=seq),
        grid=(b,),
        in_specs=[per_b(seq, 2 * KV_W), full(pe), full(w1), full(w2), full(kg0), full(seg)],
        out_specs=[per_b(nc, KV_W)] * 3,
        out_shape=(jax.ShapeDtypeStruct((b, nc, KV_W), BF16),) * 3,
        scratch_shapes=[pltpu.VMEM((2, seq + CMP_BLOCK, KV_W), F32)],
        compiler_params=pltpu.CompilerParams(dimension_semantics=("arbitrary",), vmem_limit_bytes=VMEM_LIMIT),
        name="compress",
    )(kvc, pe, w1, w2, kg0, seg)


def _loop_pairs(lo, hi, step, carry):
    pairs = (hi - lo) // 2
    carry = lax.fori_loop(0, pairs, lambda i, c: step((lo + 2 * i, lo + 2 * i + 1), c), carry)
    return lax.fori_loop(lo + 2 * pairs, hi, lambda kt, c: step((kt,), c), carry)


def _nsa_kernel(q_ref, gate_ref, kcmp_ref, vcmp0_ref, vcmp1_ref, ksel_ref, vsel0_ref, vsel1_ref, kwin_ref, vwin0_ref,
                vwin1_ref, ova_ref, hot_ref, yb_ref, *scratch, seq, n_sel, wlen, bounded, qblock):
    qb = pl.program_id(1)
    t0 = qb * qblock
    nc = seq // CMP_STRIDE
    nb = seq // SEL_BLOCK
    rows = HEADS_PER_GROUP * qblock
    groups = range(N_GROUPS)
    vcmp_refs = (vcmp0_ref, vcmp1_ref)
    vsel_refs = (vsel0_ref, vsel1_ref)
    vwin_refs = (vwin0_ref, vwin1_ref)

    q = q_ref[0]
    gates = gate_ref[0]
    lane_q = lax.broadcasted_iota(jnp.int32, (qblock, LANES), 1)
    t_col = t0 + lax.broadcasted_iota(jnp.int32, (qblock, 1), 0)
    rep = lambda a: jnp.concatenate([a] * HEADS_PER_GROUP, axis=0)

    def weights(s):
        if bounded:
            return jnp.exp2(s)
        m = jnp.maximum(jnp.max(s, axis=1, keepdims=True), 0.5 * NEG)
        return jnp.exp2(s - m)

    def inverse(l):
        return jnp.where(l > 0.0, 1.0 / l, 0.0)

    def normalized(acc):
        return acc * inverse(pltpu.roll(acc, HEAD_DIM, axis=1))

    def gated(o, g, branch):
        cols = [N_BRANCH * (g * HEADS_PER_GROUP + hl) + branch for hl in range(HEADS_PER_GROUP)]
        return jnp.concatenate([gates[:, c:c + 1] * o[hl * qblock:(hl + 1) * qblock] for hl, c in enumerate(cols)],
                               axis=0)

    cend = lax.broadcasted_iota(jnp.int32, (1, nc), 1) * CMP_STRIDE + (CMP_BLOCK - 1)
    done = rep(jnp.where(cend <= t_col, 0.0, NEG))

    qs, o_c, o_w, imp_t, lhs = [], [], [], [], []
    for g in groups:
        in_group = (lane_q >= HEAD_DIM * g) & (lane_q < HEAD_DIM * (g + 1))
        parts = []
        for hl in range(HEADS_PER_GROUP):
            pair, side = divmod(hl, 2)
            c0 = g * HEADS_PER_GROUP * HEAD_DIM + pair * LANES
            qp = q[:, c0:c0 + LANES].astype(F32)
            if side != g:
                qp = pltpu.roll(qp, HEAD_DIM, axis=1)
            parts.append(jnp.where(in_group, qp, 0.0).astype(BF16))
        qs.append(jnp.concatenate(parts, axis=0))

        p = weights(_dot_nt(qs[g], kcmp_ref[0]) + done)
        both = _dot(p.astype(BF16), jnp.concatenate([vcmp_refs[g][0], ova_ref[...]], axis=1))
        acc_v = both[:, 0:LANES]
        inv_own = inverse(pltpu.roll(acc_v, HEAD_DIM, axis=1))
        o_c.append(gated(acc_v * inv_own, g, 0))

        share = both[:, LANES:2 * LANES] * (inv_own if g == 0 else inverse(acc_v))
        imp = share[0:qblock]
        for hl in range(1, HEADS_PER_GROUP):
            imp = imp + share[hl * qblock:(hl + 1) * qblock]
        imp_t.append(imp.T[0:nb])

    def block_mask(imp_g, live):
        dead = [jnp.full((nb - live, qblock), NEG, F32)] if live < nb else []
        dead.append(jnp.zeros((LANES - nb, qblock), F32))
        if live <= n_sel:
            return jnp.concatenate([jnp.zeros((live, qblock), F32)] + dead, axis=0)
        blk = lax.broadcasted_iota(jnp.int32, (live, qblock), 0)
        cur = (t0 + lax.broadcasted_iota(jnp.int32, (live, qblock), 1)) // SEL_BLOCK
        forced = (blk == 0) | (blk == cur) | (blk == cur - 1)
        imp_a = jnp.where(forced, FORCE, jnp.where(blk <= cur, imp_g[0:live], -FORCE))
        tiles = [imp_a[r * SUBLANES:(r + 1) * SUBLANES] for r in range(live // SUBLANES)]
        sub = lax.broadcasted_iota(jnp.int32, (SUBLANES, qblock), 0)
        rank = [jnp.zeros((SUBLANES, qblock), jnp.int32) for _ in tiles]
        for jp in range(live):
            other = jnp.broadcast_to(imp_a[jp:jp + 1, :], (SUBLANES, qblock))
            for r, tile in enumerate(tiles):
                if r > jp // SUBLANES:
                    before = other >= tile
                elif r < jp // SUBLANES:
                    before = other > tile
                else:
                    before = (other > tile) | ((other == tile) & (sub > jp % SUBLANES))
                rank[r] = rank[r] + before.astype(jnp.int32)
        return jnp.concatenate([jnp.where(r < n_sel, 0.0, NEG) for r in rank] + dead, axis=0)

    live_steps = list(range(n_sel, nb + 1, n_sel))
    branches = [functools.partial(lambda live, *imps: tuple(block_mask(i, live) for i in imps), live)
                for live in live_steps]
    neg_t = lax.switch(((t0 + qblock - 1) // SEL_BLOCK) // n_sel, branches, *imp_t)
    for g in groups:
        lhs.append(jnp.concatenate([qs[g], rep(neg_t[g].T.astype(BF16))], axis=1))

    def scores(kt):
        k0 = pl.multiple_of(kt * KEY_TILE, KEY_TILE)
        rhs = jnp.concatenate([ksel_ref[0, pl.ds(k0, KEY_TILE), :], hot_ref[pl.ds(k0, KEY_TILE), :]], axis=1)
        kpos = k0 + lax.broadcasted_iota(jnp.int32, (1, KEY_TILE), 1)
        causal = rep(jnp.where(kpos <= t_col, 0.0, NEG))
        return [_dot_nt(lhs[g], rhs) + causal for g in groups]

    def values(kt, g):
        return vsel_refs[g][0, pl.ds(pl.multiple_of(kt * KEY_TILE, KEY_TILE), KEY_TILE), :]

    n_tiles = (t0 + qblock - 1) // KEY_TILE + 1
    zeros = tuple(jnp.zeros((rows, LANES), F32) for _ in groups)

    if bounded:
        def sweep(tiles, acc):
            s_t = [scores(kt) for kt in tiles]
            for kt, s_kt in zip(tiles, s_t):
                acc = tuple(acc[g] + _dot(jnp.exp2(s_kt[g]).astype(BF16), values(kt, g)) for g in groups)
            return acc

        acc_s = _loop_pairs(0, n_tiles, sweep, zeros)
    else:
        sbuf, = scratch

        lane_tiles = [slice(c * LANES, (c + 1) * LANES) for c in range(KEY_TILE // LANES)]

        def fold_max(m_run, s_t):
            return functools.reduce(jnp.maximum, [s_t[:, c] for c in lane_tiles], m_run)

        def pass1(tiles, m_run):
            for kt in tiles:
                s_t = scores(kt)
                for g in groups:
                    sbuf[g, kt] = s_t[g]
                m_run = tuple(fold_max(m_run[g], s_t[g]) for g in groups)
            return m_run

        m_run = _loop_pairs(0, n_tiles, pass1, tuple(jnp.full((rows, LANES), NEG, F32) for _ in groups))
        m_sel = [jnp.broadcast_to(jnp.max(m_run[g], axis=1, keepdims=True), (rows, LANES)) for g in groups]

        def pass2(tiles, acc):
            for kt in tiles:
                out = []
                for g in groups:
                    s_g = sbuf[g, kt]
                    p = jnp.concatenate([jnp.exp2(s_g[:, c] - m_sel[g]) for c in lane_tiles], axis=1).astype(BF16)
                    out.append(acc[g] + _dot(p, values(kt, g)))
                acc = tuple(out)
            return acc

        acc_s = _loop_pairs(0, n_tiles, pass2, zeros)

    start = pl.multiple_of(jnp.maximum(t0 + qblock - wlen, 0), qblock)
    diff = t_col - (start + lax.broadcasted_iota(jnp.int32, (1, wlen), 1))
    band = rep(jnp.where((diff >= 0) & (diff < WINDOW), 0.0, NEG))
    kw = kwin_ref[0, pl.ds(start, wlen), :]
    for g in groups:
        p = weights(_dot_nt(qs[g], kw) + band)
        o_w.append(gated(normalized(_dot(p.astype(BF16), vwin_refs[g][0, pl.ds(start, wlen), :])), g, 2))

    for g in groups:
        out = o_c[g] + gated(normalized(acc_s[g]), g, 1) + o_w[g]
        for pair in range(HEADS_PER_GROUP // 2):
            left = out[2 * pair * qblock:(2 * pair + 1) * qblock]
            right = out[(2 * pair + 1) * qblock:(2 * pair + 2) * qblock]
            if g == 0:
                right = pltpu.roll(right, HEAD_DIM, axis=1)
            else:
                left = pltpu.roll(left, HEAD_DIM, axis=1)
            c0 = g * HEADS_PER_GROUP * HEAD_DIM + pair * LANES
            yb_ref[0, :, c0:c0 + LANES] = jnp.where(lane_q < HEAD_DIM, left, right).astype(BF16)


def _nsa_call(bounded, q, gates, kcmp, vcmp0, vcmp1, ksl, vsl0, vsl1, kwn, vwn0, vwn1, ova, hot):
    b, seq, _ = q.shape
    nc = seq // CMP_STRIDE
    qblock = min(SWEEP_Q_BLOCK if bounded else Q_BLOCK, seq)
    wlen = min(WINDOW + qblock, seq)
    n_sel = min(N_SEL, seq // SEL_BLOCK)
    qblk = lambda w: pl.BlockSpec((1, qblock, w), lambda i, j: (i, j, 0))
    per_b = lambda rows: pl.BlockSpec((1, rows, KV_W), lambda i, j: (i, 0, 0))
    full = lambda a: pl.BlockSpec(a.shape, lambda i, j: (0,) * a.ndim)
    score_buf = pltpu.VMEM((N_GROUPS, seq // KEY_TILE, HEADS_PER_GROUP * qblock, KEY_TILE), F32)
    return pl.pallas_call(
        functools.partial(_nsa_kernel, seq=seq, n_sel=n_sel, wlen=wlen, bounded=bounded, qblock=qblock),
        grid=(b, seq // qblock),
        in_specs=[qblk(ATTN_W), qblk(GATE_PAD)] + [per_b(nc)] * 3 + [per_b(seq)] * 6 + [full(ova), full(hot)],
        out_specs=qblk(ATTN_W),
        out_shape=jax.ShapeDtypeStruct((b, seq, ATTN_W), BF16),
        scratch_shapes=[] if bounded else [score_buf],
        compiler_params=pltpu.CompilerParams(dimension_semantics=("arbitrary", "arbitrary"),
                                             vmem_limit_bytes=VMEM_LIMIT),
        name="nsa_bounded" if bounded else "nsa",
    )(q, gates, kcmp, vcmp0, vcmp1, ksl, vsl0, vsl1, kwn, vwn0, vwn1, ova, hot)


_FF_CHUNKS = ((0, 1024), (1024, 1024), (2048, 512), (2560, 256))


def _mix_ffn_kernel(x_ref, ya_ref, yb_ref, gm_ref, wa_ref, wb_ref, wo_ref, g2_ref, win_ref, cw_ref, cb_ref, wout_ref,
                    out_ref, gbuf, *, tiles_per_seq, tm):
    i = pl.program_id(0)

    @pl.when(i % tiles_per_seq == 0)
    def _():
        gbuf[0:SUBLANES, :] = jnp.zeros((SUBLANES, D_FF), F32)

    gm = gm_ref[...].astype(F32)
    mix = gm[:, 0:D_MODEL] * _dot(ya_ref[...], wa_ref[...]) + gm[:, D_MODEL:] * _dot(yb_ref[...], wb_ref[...])
    x = x_ref[...] + _dot(mix.astype(BF16), wo_ref[...])

    ms = jnp.mean(x * x, axis=-1, keepdims=True)
    h = (x * lax.rsqrt(ms + EPS) * g2_ref[...]).astype(BF16)

    def project(c0, w):
        return _dot(h, win_ref[:, c0:c0 + w]), _dot(h, win_ref[:, D_FF + c0:D_FF + c0 + w])

    acc = jnp.zeros((tm, D_MODEL), F32)
    ahead = project(*_FF_CHUNKS[0])
    for i, (c0, w) in enumerate(_FF_CHUNKS):
        cols = slice(c0, c0 + w)
        gpre, up = ahead
        if i + 1 < len(_FF_CHUNKS):
            ahead = project(*_FF_CHUNKS[i + 1])
        gbuf[SUBLANES:SUBLANES + tm, cols] = gpre
        cw = cw_ref[:, cols]
        conv = (cw[0:1, :] * gbuf[SUBLANES - 2:SUBLANES - 2 + tm, cols]
                + cw[1:2, :] * gbuf[SUBLANES - 1:SUBLANES - 1 + tm, cols]
                + cw[2:3, :] * gpre) + cb_ref[:, cols]
        gbuf[0:SUBLANES, cols] = gbuf[tm:tm + SUBLANES, cols]
        act = conv * jax.nn.sigmoid(conv) * up
        acc = acc + _dot(act.astype(BF16), wout_ref[c0:c0 + w, :])
    out_ref[...] = x + acc


def _mix_ffn_call(x2, ya, yb, gm, wa, wb, wo, g2, win, conv_w, conv_b, wout, seq):
    n = x2.shape[0]
    tm = min(ROW_TILE, seq)
    row = lambda w: pl.BlockSpec((tm, w), lambda i: (i, 0))
    full = lambda a: pl.BlockSpec(a.shape, lambda i: (0,) * a.ndim)
    once = lambda a: pl.BlockSpec(a.shape, lambda i: (0,) * a.ndim, pipeline_mode=pl.Buffered(1))
    return pl.pallas_call(
        functools.partial(_mix_ffn_kernel, tiles_per_seq=seq // tm, tm=tm),
        grid=(n // tm,),
        in_specs=[row(D_MODEL), row(CONV_W), row(ATTN_W), row(2 * D_MODEL), once(wa), once(wb), once(wo), full(g2),
                  once(win), full(conv_w), full(conv_b), once(wout)],
        out_specs=row(D_MODEL),
        out_shape=jax.ShapeDtypeStruct((n, D_MODEL), F32),
        scratch_shapes=[pltpu.VMEM((tm + SUBLANES, D_FF), F32)],
        compiler_params=pltpu.CompilerParams(dimension_semantics=("arbitrary",), vmem_limit_bytes=VMEM_LIMIT),
        name="mix_ffn",
    )(x2, ya, yb, gm, wa, wb, wo, g2, win, conv_w, conv_b, wout)


def _block_diag(*blocks):
    rows = []
    for i, blk in enumerate(blocks):
        rows.append(jnp.concatenate([blk if j == i else jnp.zeros_like(blk) for j in range(len(blocks))], axis=-1))
    return jnp.concatenate(rows, axis=-2)


def _static_tables(seq):
    nc, nb = seq // CMP_STRIDE, seq // SEL_BLOCK
    n_cmp = (seq - CMP_BLOCK) // CMP_STRIDE + 1
    ii = np.arange(nc)[None, :] * CMP_STRIDE
    jj = np.arange(nb)[:, None] * SEL_BLOCK
    ovt = ((ii < jj + SEL_BLOCK) & (ii + CMP_BLOCK > jj) & (np.arange(nc)[None, :] < n_cmp)).astype(np.float32)
    ova = np.ones((nc, LANES), np.float32)
    ova[:, :HEAD_DIM] = 0.0
    ova[:, :nb] = ovt.T
    hot = (np.arange(LANES)[None, :] == (np.arange(seq) // SEL_BLOCK)[:, None]).astype(np.float32)
    head = np.arange(SEG_W) // HEAD_DIM
    seg = (head[:, None] == head[None, :]).astype(np.float32)
    return jnp.asarray(ova, BF16), jnp.asarray(hot, BF16), jnp.asarray(seg, BF16)


def kernel(x, positions, norm1_g, w_in, conv_a_w, w_a_out, q_norm_g, k_norm_g, cmp_k_pe, cmp_k_w1, cmp_k_w2, cmp_v_pe, cmp_v_w1, cmp_v_w2, w_b_out, w_o, norm2_g, w_ffn_in, ffn_conv_w, ffn_conv_b, w_ffn_out):
    b, seq, d = x.shape
    n = b * seq
    half = HEAD_DIM // 2
    inv = ROPE_THETA ** (-jnp.arange(half, dtype=F32) / half)
    inv_lanes = jnp.tile(inv, LANES // half)[None, :]
    tm = min(ROW_TILE, seq)
    pos3 = positions.astype(F32).reshape(n // tm, tm // LANES, LANES)
    r_ang = jnp.arange(tm, dtype=F32)[:, None] * inv_lanes
    rcos, rsin = jnp.cos(r_ang), jnp.sin(r_ang)
    stepwise = jnp.all(pos3.reshape(n // tm, tm) == pos3[:, 0, 0:1] + jnp.arange(tm, dtype=F32)[None, :])
    ova, hot, seg = _static_tables(seq)
    gate_cols = N_BRANCH * N_HEADS

    x2 = x.reshape(n, d)
    for l in range(w_in.shape[0]):
        w = w_in[l]
        w_main = w[:, :_GATE_OFF].astype(BF16)
        w_gate = jnp.pad(w[:, _GATE_OFF:_GATE_OFF + gate_cols], ((0, 0), (0, GATE_PAD - gate_cols))).astype(BF16)
        w_gm = w[:, _GATE_OFF + gate_cols:].astype(BF16)
        qg = jnp.tile(q_norm_g[l], N_HEADS)[None, :]
        kg = jnp.tile(k_norm_g[l], (1, N_GROUPS))
        ya, q, kvc, ksl, vsl0, vsl1, kwn, vwn0, vwn1, gates, gm = lax.cond(
            stepwise, functools.partial(_proj_call, True, seq), functools.partial(_proj_call, False, seq),
            x2, pos3, norm1_g[l][None, :], w_main, w_gate, w_gm, conv_a_w[l], qg, kg[1:3].reshape(1, 2 * KV_W),
            inv_lanes, rcos, rsin, seg)

        per_b = lambda a: a.reshape(b, seq, a.shape[-1])
        w1k, w1v = (w.reshape(CMP_BLOCK, HEAD_DIM, HEAD_DIM) for w in (cmp_k_w1[l], cmp_v_w1[l]))
        kcmp, vcmp0, vcmp1 = _compress_call(
            per_b(kvc), jnp.concatenate([jnp.tile(cmp_k_pe[l], (1, N_GROUPS)), jnp.tile(cmp_v_pe[l], (1, N_GROUPS))], axis=1),
            _block_diag(w1k, w1k, w1v, w1v).astype(BF16),
            _block_diag(cmp_k_w2[l], cmp_k_w2[l], cmp_v_w2[l], cmp_v_w2[l]).astype(BF16), kg[0:1], seg)

        reach = HEAD_DIM * (HEAD_DIM ** -0.5 * LOG2E) * jnp.max(jnp.abs(q_norm_g[l])) * jnp.max(jnp.abs(k_norm_g[l]))
        nsa_args = (per_b(q), per_b(gates), kcmp, vcmp0, vcmp1, per_b(ksl), per_b(vsl0), per_b(vsl1), per_b(kwn),
                    per_b(vwn0), per_b(vwn1), ova, hot)
        yb = lax.cond(reach * 1.02 <= SCORE_BOUND, functools.partial(_nsa_call, True),
                      functools.partial(_nsa_call, False), *nsa_args)

        x2 = _mix_ffn_call(x2, ya, yb.reshape(n, ATTN_W), gm, w_a_out[l].astype(BF16), w_b_out[l].astype(BF16),
                           w_o[l].astype(BF16), norm2_g[l][None, :], w_ffn_in[l].astype(BF16), ffn_conv_w[l],
                           ffn_conv_b[l][None, :], w_ffn_out[l].astype(BF16), seq)
    return x2.reshape(b, seq, d)
```

```python
import functools

import jax
import jax.numpy as jnp
import numpy as np
from jax import lax
from jax.experimental import pallas as pl
from jax.experimental.pallas import tpu as pltpu

F32 = jnp.float32
BF16 = jnp.bfloat16

D_MODEL = 1024
CONV_W = 512
CONV_K = 3
N_HEADS = 8
HEAD_DIM = 64
N_GROUPS = 2
HEADS_PER_GROUP = N_HEADS // N_GROUPS
ATTN_W = N_HEADS * HEAD_DIM
KV_W = N_GROUPS * HEAD_DIM
CMP_BLOCK = 32
CMP_STRIDE = 16
SEL_BLOCK = 64
N_SEL = 16
WINDOW = 512
Q_BLOCK = 128
SWEEP_Q_BLOCK = 256
N_BRANCH = 3
ROPE_THETA = 10000.0
D_FF = 2816
EPS = 1e-6
NEG = -1e30
FORCE = 1e9
LOG2E = 1.4426950408889634
SCORE_BOUND = 60.0

LANES = 128
SUBLANES = 8
KEY_TILE = 512
SEG_W = 256
ROW_TILE = 512
GATE_PAD = LANES
V7X_VMEM_BYTES = 64 * 1024 * 1024
VMEM_LIMIT = V7X_VMEM_BYTES * 7 // 8

_A_END = 3 * CONV_W
_Q_OFF = _A_END
_KV_OFF = _Q_OFF + ATTN_W
_GATE_OFF = _KV_OFF + 6 * KV_W


def _dot(a, b):
    return jnp.dot(a, b, preferred_element_type=F32)


def _dot_nt(a, b):
    return lax.dot_general(a, b, (((1,), (1,)), ((), ())), preferred_element_type=F32)


def _head_norm(x, gain, seg):
    sq = x * x
    hi = sq.astype(BF16)
    lo = (sq - hi.astype(F32)).astype(BF16)
    ms = (_dot(hi, seg) + _dot(lo, seg)) * (1.0 / HEAD_DIM)
    return x * lax.rsqrt(ms + EPS) * gain


def _rope(x, cos, sin_signed):
    lane = lax.broadcasted_iota(jnp.int32, x.shape, 1)
    half = HEAD_DIM // 2
    rot = jnp.where((lane & half) != 0, pltpu.roll(x, half, axis=1), pltpu.roll(x, LANES - half, axis=1))
    return x * cos + rot * sin_signed


def _proj_kernel(x_ref, pos_ref, g1_ref, w_ref, wgate_ref, wgm_ref, cw_ref, qg_ref, kg_ref, inv_ref, rcos_ref,
                 rsin_ref, seg_ref,
                 ya_ref, q_ref, kvc_ref, ksl_ref, vsl0_ref, vsl1_ref, kwn_ref, vwn0_ref, vwn1_ref, gate_ref,
                 gm_ref, ubuf, *, tiles_per_seq, tm, stepwise):
    i = pl.program_id(0)

    @pl.when(i % tiles_per_seq == 0)
    def _():
        ubuf[0:SUBLANES, :] = jnp.zeros((SUBLANES, CONV_W), F32)

    x = x_ref[...]
    ms = jnp.mean(x * x, axis=-1, keepdims=True)
    h = (x * lax.rsqrt(ms + EPS) * g1_ref[...]).astype(BF16)

    pa = _dot(h, w_ref[:, 0:_A_END])
    u = pa[:, CONV_W:2 * CONV_W] * pa[:, 2 * CONV_W:3 * CONV_W]
    ubuf[SUBLANES:SUBLANES + tm, :] = u
    cw = cw_ref[...]
    y = (cw[0:1, :] * ubuf[SUBLANES - 2:SUBLANES - 2 + tm, :]
         + cw[1:2, :] * ubuf[SUBLANES - 1:SUBLANES - 1 + tm, :]
         + cw[2:3, :] * u)
    ya_ref[...] = (pa[:, 0:CONV_W] * y).astype(BF16)
    ubuf[0:SUBLANES, :] = ubuf[tm:tm + SUBLANES, :]

    pos = pos_ref[0]
    inv = inv_ref[...]
    if stepwise:
        p0 = pos[0:1, 0:1]
        ca, sa = jnp.cos(p0 * inv), jnp.sin(p0 * inv)
        cb, sb = rcos_ref[...], rsin_ref[...]
        cos, sin = ca * cb - sa * sb, sa * cb + ca * sb
    else:
        cols = [jnp.broadcast_to(pos[c:c + 1, :], (LANES, LANES)).T for c in range(tm // LANES)]
        ang = jnp.concatenate(cols, axis=0) * inv
        cos, sin = jnp.cos(ang), jnp.sin(ang)
    lane = lax.broadcasted_iota(jnp.int32, cos.shape, 1)
    sin_signed = jnp.where((lane & (HEAD_DIM // 2)) != 0, sin, -sin)

    gm_ref[...] = jax.nn.sigmoid(_dot(h, wgm_ref[...])).astype(BF16)
    pb = _dot(h, w_ref[:, _Q_OFF:_GATE_OFF])
    scale = HEAD_DIM ** -0.5 * LOG2E
    seg = seg_ref[...]
    for c in range(ATTN_W // SEG_W):
        cols = slice(c * SEG_W, (c + 1) * SEG_W)
        qn = _head_norm(pb[:, cols], qg_ref[:, cols], seg)
        for t in range(SEG_W // LANES):
            lanes = slice(t * LANES, (t + 1) * LANES)
            q_ref[:, c * SEG_W + t * LANES:c * SEG_W + (t + 1) * LANES] = (
                _rope(qn[:, lanes], cos, sin_signed) * scale).astype(BF16)
    kv = pb[:, ATTN_W:ATTN_W + 6 * KV_W]
    kvc_ref[...] = kv[:, 0:2 * KV_W]
    kn = _head_norm(jnp.concatenate([kv[:, 2 * KV_W:3 * KV_W], kv[:, 4 * KV_W:5 * KV_W]], axis=1), kg_ref[...], seg)
    ksl_ref[...] = _rope(kn[:, 0:KV_W], cos, sin_signed).astype(BF16)
    kwn_ref[...] = _rope(kn[:, KV_W:2 * KV_W], cos, sin_signed).astype(BF16)
    first = lane < HEAD_DIM
    for v, out0, out1 in ((kv[:, 3 * KV_W:4 * KV_W], vsl0_ref, vsl1_ref), (kv[:, 5 * KV_W:6 * KV_W], vwn0_ref, vwn1_ref)):
        out0[...] = jnp.where(first, v, 1.0).astype(BF16)
        out1[...] = jnp.where(first, 1.0, v).astype(BF16)
    gate_ref[...] = jax.nn.sigmoid(_dot(h, wgate_ref[...]))


def _proj_call(stepwise, seq, x2, pos3, g1, w_main, w_gate, w_gm, conv_w, qg, kg, inv, rcos, rsin, seg):
    n = x2.shape[0]
    tm = min(ROW_TILE, seq)
    row = lambda w: pl.BlockSpec((tm, w), lambda i: (i, 0))
    full = lambda a: pl.BlockSpec(a.shape, lambda i: (0,) * a.ndim)
    once = lambda a: pl.BlockSpec(a.shape, lambda i: (0,) * a.ndim, pipeline_mode=pl.Buffered(1))
    out_shapes = (
        jax.ShapeDtypeStruct((n, CONV_W), BF16),
        jax.ShapeDtypeStruct((n, ATTN_W), BF16),
        jax.ShapeDtypeStruct((n, 2 * KV_W), F32),
        jax.ShapeDtypeStruct((n, KV_W), BF16),
        jax.ShapeDtypeStruct((n, KV_W), BF16),
        jax.ShapeDtypeStruct((n, KV_W), BF16),
        jax.ShapeDtypeStruct((n, KV_W), BF16),
        jax.ShapeDtypeStruct((n, KV_W), BF16),
        jax.ShapeDtypeStruct((n, KV_W), BF16),
        jax.ShapeDtypeStruct((n, GATE_PAD), F32),
        jax.ShapeDtypeStruct((n, 2 * D_MODEL), BF16),
    )
    return pl.pallas_call(
        functools.partial(_proj_kernel, tiles_per_seq=seq // tm, tm=tm, stepwise=stepwise),
        grid=(n // tm,),
        in_specs=[row(D_MODEL), pl.BlockSpec((1, tm // LANES, LANES), lambda i: (i, 0, 0)), full(g1), once(w_main),
                  once(w_gate), once(w_gm), full(conv_w), full(qg), full(kg), full(inv), full(rcos), full(rsin), full(seg)],
        out_specs=[row(s.shape[1]) for s in out_shapes],
        out_shape=out_shapes,
        scratch_shapes=[pltpu.VMEM((tm + SUBLANES, CONV_W), F32)],
        compiler_params=pltpu.CompilerParams(dimension_semantics=("arbitrary",), vmem_limit_bytes=VMEM_LIMIT),
        name="proj_stepwise" if stepwise else "proj",
    )(x2, pos3, g1, w_main, w_gate, w_gm, conv_w, qg, kg, inv, rcos, rsin, seg)


def _compress_kernel(kvc_ref, pe_ref, w1_ref, w2_ref, kg_ref, seg_ref, kcmp_ref, vcmp0_ref, vcmp1_ref, buf, *, seq):
    nc = seq // CMP_STRIDE
    for part in range(2):
        buf[part, 0:seq, :] = kvc_ref[0, :, part * KV_W:(part + 1) * KV_W]
        buf[part, seq:seq + CMP_BLOCK, :] = jnp.zeros((CMP_BLOCK, KV_W), F32)
    acc = jnp.zeros((nc, 2 * KV_W), F32)
    for r in range(CMP_BLOCK):
        xr = jnp.concatenate([buf[part, pl.ds(r, nc, stride=CMP_STRIDE), :] for part in range(2)], axis=1)
        acc = acc + _dot((xr + pe_ref[r:r + 1, :]).astype(BF16), w1_ref[r])
    hmid = acc * jax.nn.sigmoid(acc)
    o = _dot(hmid.astype(BF16), w2_ref[...])
    kcmp_ref[0] = _head_norm(o[:, 0:KV_W], kg_ref[...], seg_ref[0:KV_W, 0:KV_W]).astype(BF16)
    v = o[:, KV_W:2 * KV_W]
    first = lax.broadcasted_iota(jnp.int32, v.shape, 1) < HEAD_DIM
    vcmp0_ref[0] = jnp.where(first, v, 1.0).astype(BF16)
    vcmp1_ref[0] = jnp.where(first, 1.0, v).astype(BF16)


def _compress_call(kvc, pe, w1, w2, kg0, seg):
    b, seq, _ = kvc.shape
    nc = seq // CMP_STRIDE
    per_b = lambda rows, w: pl.BlockSpec((1, rows, w), lambda i: (i, 0, 0))
    full = lambda a: pl.BlockSpec(a.shape, lambda i: (0,) * a.ndim)
    return pl.pallas_call(
        functools.partial(_compress_kernel, seq=seq),
        grid=(b,),
        in_specs=[per_b(seq, 2 * KV_W), full(pe), full(w1), full(w2), full(kg0), full(seg)],
        out_specs=[per_b(nc, KV_W)] * 3,
        out_shape=(jax.ShapeDtypeStruct((b, nc, KV_W), BF16),) * 3,
        scratch_shapes=[pltpu.VMEM((2, seq + CMP_BLOCK, KV_W), F32)],
        compiler_params=pltpu.CompilerParams(dimension_semantics=("arbitrary",), vmem_limit_bytes=VMEM_LIMIT),
        name="compress",
    )(kvc, pe, w1, w2, kg0, seg)


def _loop_pairs(lo, hi, step, carry):
    pairs = (hi - lo) // 2
    carry = lax.fori_loop(0, pairs, lambda i, c: step((lo + 2 * i, lo + 2 * i + 1), c), carry)
    return lax.fori_loop(lo + 2 * pairs, hi, lambda kt, c: step((kt,), c), carry)


def _nsa_kernel(q_ref, gate_ref, kcmp_ref, vcmp0_ref, vcmp1_ref, ksel_ref, vsel0_ref, vsel1_ref, kwin_ref, vwin0_ref,
                vwin1_ref, ova_ref, hot_ref, yb_ref, *scratch, seq, n_sel, wlen, bounded, qblock):
    qb = pl.program_id(1)
    t0 = qb * qblock
    nc = seq // CMP_STRIDE
    nb = seq // SEL_BLOCK
    rows = HEADS_PER_GROUP * qblock
    groups = range(N_GROUPS)
    vcmp_refs = (vcmp0_ref, vcmp1_ref)
    vsel_refs = (vsel0_ref, vsel1_ref)
    vwin_refs = (vwin0_ref, vwin1_ref)

    q = q_ref[0]
    gates = gate_ref[0]
    lane_q = lax.broadcasted_iota(jnp.int32, (qblock, LANES), 1)
    t_col = t0 + lax.broadcasted_iota(jnp.int32, (qblock, 1), 0)
    rep = lambda a: jnp.concatenate([a] * HEADS_PER_GROUP, axis=0)

    def weights(s):
        if bounded:
            return jnp.exp2(s)
        m = jnp.maximum(jnp.max(s, axis=1, keepdims=True), 0.5 * NEG)
        return jnp.exp2(s - m)

    def inverse(l):
        return jnp.where(l > 0.0, 1.0 / l, 0.0)

    def normalized(acc):
        return acc * inverse(pltpu.roll(acc, HEAD_DIM, axis=1))

    def gated(o, g, branch):
        cols = [N_BRANCH * (g * HEADS_PER_GROUP + hl) + branch for hl in range(HEADS_PER_GROUP)]
        return jnp.concatenate([gates[:, c:c + 1] * o[hl * qblock:(hl + 1) * qblock] for hl, c in enumerate(cols)],
                               axis=0)

    cend = lax.broadcasted_iota(jnp.int32, (1, nc), 1) * CMP_STRIDE + (CMP_BLOCK - 1)
    done = rep(jnp.where(cend <= t_col, 0.0, NEG))

    qs, o_c, o_w, imp_t, lhs = [], [], [], [], []
    for g in groups:
        in_group = (lane_q >= HEAD_DIM * g) & (lane_q < HEAD_DIM * (g + 1))
        parts = []
        for hl in range(HEADS_PER_GROUP):
            pair, side = divmod(hl, 2)
            c0 = g * HEADS_PER_GROUP * HEAD_DIM + pair * LANES
            qp = q[:, c0:c0 + LANES].astype(F32)
            if side != g:
                qp = pltpu.roll(qp, HEAD_DIM, axis=1)
            parts.append(jnp.where(in_group, qp, 0.0).astype(BF16))
        qs.append(jnp.concatenate(parts, axis=0))

        p = weights(_dot_nt(qs[g], kcmp_ref[0]) + done)
        both = _dot(p.astype(BF16), jnp.concatenate([vcmp_refs[g][0], ova_ref[...]], axis=1))
        acc_v = both[:, 0:LANES]
        inv_own = inverse(pltpu.roll(acc_v, HEAD_DIM, axis=1))
        o_c.append(gated(acc_v * inv_own, g, 0))

        share = both[:, LANES:2 * LANES] * (inv_own if g == 0 else inverse(acc_v))
        imp = share[0:qblock]
        for hl in range(1, HEADS_PER_GROUP):
            imp = imp + share[hl * qblock:(hl + 1) * qblock]
        imp_t.append(imp.T[0:nb])

    def block_mask(imp_g, live):
        dead = [jnp.full((nb - live, qblock), NEG, F32)] if live < nb else []
        dead.append(jnp.zeros((LANES - nb, qblock), F32))
        if live <= n_sel:
            return jnp.concatenate([jnp.zeros((live, qblock), F32)] + dead, axis=0)
        blk = lax.broadcasted_iota(jnp.int32, (live, qblock), 0)
        cur = (t0 + lax.broadcasted_iota(jnp.int32, (live, qblock), 1)) // SEL_BLOCK
        forced = (blk == 0) | (blk == cur) | (blk == cur - 1)
        imp_a = jnp.where(forced, FORCE, jnp.where(blk <= cur, imp_g[0:live], -FORCE))
        tiles = [imp_a[r * SUBLANES:(r + 1) * SUBLANES] for r in range(live // SUBLANES)]
        sub = lax.broadcasted_iota(jnp.int32, (SUBLANES, qblock), 0)
        rank = [jnp.zeros((SUBLANES, qblock), jnp.int32) for _ in tiles]
        for jp in range(live):
            other = jnp.broadcast_to(imp_a[jp:jp + 1, :], (SUBLANES, qblock))
            for r, tile in enumerate(tiles):
                if r > jp // SUBLANES:
                    before = other >= tile
                elif r < jp // SUBLANES:
                    before = other > tile
                else:
                    before = (other > tile) | ((other == tile) & (sub > jp % SUBLANES))
                rank[r] = rank[r] + before.astype(jnp.int32)
        return jnp.concatenate([jnp.where(r < n_sel, 0.0, NEG) for r in rank] + dead, axis=0)

    live_steps = list(range(n_sel, nb + 1, n_sel))
    branches = [functools.partial(lambda live, *imps: tuple(block_mask(i, live) for i in imps), live)
                for live in live_steps]
    neg_t = lax.switch(((t0 + qblock - 1) // SEL_BLOCK) // n_sel, branches, *imp_t)
    for g in groups:
        lhs.append(jnp.concatenate([qs[g], rep(neg_t[g].T.astype(BF16))], axis=1))

    def scores(kt):
        k0 = pl.multiple_of(kt * KEY_TILE, KEY_TILE)
        rhs = jnp.concatenate([ksel_ref[0, pl.ds(k0, KEY_TILE), :], hot_ref[pl.ds(k0, KEY_TILE), :]], axis=1)
        kpos = k0 + lax.broadcasted_iota(jnp.int32, (1, KEY_TILE), 1)
        causal = rep(jnp.where(kpos <= t_col, 0.0, NEG))
        return [_dot_nt(lhs[g], rhs) + causal for g in groups]

    def values(kt, g):
        return vsel_refs[g][0, pl.ds(pl.multiple_of(kt * KEY_TILE, KEY_TILE), KEY_TILE), :]

    n_tiles = (t0 + qblock - 1) // KEY_TILE + 1
    zeros = tuple(jnp.zeros((rows, LANES), F32) for _ in groups)

    if bounded:
        def sweep(tiles, acc):
            s_t = [scores(kt) for kt in tiles]
            for kt, s_kt in zip(tiles, s_t):
                acc = tuple(acc[g] + _dot(jnp.exp2(s_kt[g]).astype(BF16), values(kt, g)) for g in groups)
            return acc

        acc_s = _loop_pairs(0, n_tiles, sweep, zeros)
    else:
        sbuf, = scratch

        lane_tiles = [slice(c * LANES, (c + 1) * LANES) for c in range(KEY_TILE // LANES)]

        def fold_max(m_run, s_t):
            return functools.reduce(jnp.maximum, [s_t[:, c] for c in lane_tiles], m_run)

        def pass1(tiles, m_run):
            for kt in tiles:
                s_t = scores(kt)
                for g in groups:
                    sbuf[g, kt] = s_t[g]
                m_run = tuple(fold_max(m_run[g], s_t[g]) for g in groups)
            return m_run

        m_run = _loop_pairs(0, n_tiles, pass1, tuple(jnp.full((rows, LANES), NEG, F32) for _ in groups))
        m_sel = [jnp.broadcast_to(jnp.max(m_run[g], axis=1, keepdims=True), (rows, LANES)) for g in groups]

        def pass2(tiles, acc):
            for kt in tiles:
                out = []
                for g in groups:
                    s_g = sbuf[g, kt]
                    p = jnp.concatenate([jnp.exp2(s_g[:, c] - m_sel[g]) for c in lane_tiles], axis=1).astype(BF16)
                    out.append(acc[g] + _dot(p, values(kt, g)))
                acc = tuple(out)
            return acc

        acc_s = _loop_pairs(0, n_tiles, pass2, zeros)

    start = pl.multiple_of(jnp.maximum(t0 + qblock - wlen, 0), qblock)
    diff = t_col - (start + lax.broadcasted_iota(jnp.int32, (1, wlen), 1))
    band = rep(jnp.where((diff >= 0) & (diff < WINDOW), 0.0, NEG))
    kw = kwin_ref[0, pl.ds(start, wlen), :]
    for g in groups:
        p = weights(_dot_nt(qs[g], kw) + band)
        o_w.append(gated(normalized(_dot(p.astype(BF16), vwin_refs[g][0, pl.ds(start, wlen), :])), g, 2))

    for g in groups:
        out = o_c[g] + gated(normalized(acc_s[g]), g, 1) + o_w[g]
        for pair in range(HEADS_PER_GROUP // 2):
            left = out[2 * pair * qblock:(2 * pair + 1) * qblock]
            right = out[(2 * pair + 1) * qblock:(2 * pair + 2) * qblock]
            if g == 0:
                right = pltpu.roll(right, HEAD_DIM, axis=1)
            else:
                left = pltpu.roll(left, HEAD_DIM, axis=1)
            c0 = g * HEADS_PER_GROUP * HEAD_DIM + pair * LANES
            yb_ref[0, :, c0:c0 + LANES] = jnp.where(lane_q < HEAD_DIM, left, right).astype(BF16)


def _nsa_call(bounded, q, gates, kcmp, vcmp0, vcmp1, ksl, vsl0, vsl1, kwn, vwn0, vwn1, ova, hot):
    b, seq, _ = q.shape
    nc = seq // CMP_STRIDE
    qblock = min(SWEEP_Q_BLOCK if bounded else Q_BLOCK, seq)
    wlen = min(WINDOW + qblock, seq)
    n_sel = min(N_SEL, seq // SEL_BLOCK)
    qblk = lambda w: pl.BlockSpec((1, qblock, w), lambda i, j: (i, j, 0))
    per_b = lambda rows: pl.BlockSpec((1, rows, KV_W), lambda i, j: (i, 0, 0))
    full = lambda a: pl.BlockSpec(a.shape, lambda i, j: (0,) * a.ndim)
    score_buf = pltpu.VMEM((N_GROUPS, seq // KEY_TILE, HEADS_PER_GROUP * qblock, KEY_TILE), F32)
    return pl.pallas_call(
        functools.partial(_nsa_kernel, seq=seq, n_sel=n_sel, wlen=wlen, bounded=bounded, qblock=qblock),
        grid=(b, seq // qblock),
        in_specs=[qblk(ATTN_W), qblk(GATE_PAD)] + [per_b(nc)] * 3 + [per_b(seq)] * 6 + [full(ova), full(hot)],
        out_specs=qblk(ATTN_W),
        out_shape=jax.ShapeDtypeStruct((b, seq, ATTN_W), BF16),
        scratch_shapes=[] if bounded else [score_buf],
        compiler_params=pltpu.CompilerParams(dimension_semantics=("arbitrary", "arbitrary"),
                                             vmem_limit_bytes=VMEM_LIMIT),
        name="nsa_bounded" if bounded else "nsa",
    )(q, gates, kcmp, vcmp0, vcmp1, ksl, vsl0, vsl1, kwn, vwn0, vwn1, ova, hot)


_FF_CHUNKS = ((0, 1024), (1024, 1024), (2048, 512), (2560, 256))


def _mix_ffn_kernel(x_ref, ya_ref, yb_ref, gm_ref, wa_ref, wb_ref, wo_ref, g2_ref, win_ref, cw_ref, cb_ref, wout_ref,
                    out_ref, gbuf, *, tiles_per_seq, tm):
    i = pl.program_id(0)

    @pl.when(i % tiles_per_seq == 0)
    def _():
        gbuf[0:SUBLANES, :] = jnp.zeros((SUBLANES, D_FF), F32)

    gm = gm_ref[...].astype(F32)
    mix = gm[:, 0:D_MODEL] * _dot(ya_ref[...], wa_ref[...]) + gm[:, D_MODEL:] * _dot(yb_ref[...], wb_ref[...])
    x = x_ref[...] + _dot(mix.astype(BF16), wo_ref[...])

    ms = jnp.mean(x * x, axis=-1, keepdims=True)
    h = (x * lax.rsqrt(ms + EPS) * g2_ref[...]).astype(BF16)

    def project(c0, w):
        return _dot(h, win_ref[:, c0:c0 + w]), _dot(h, win_ref[:, D_FF + c0:D_FF + c0 + w])

    acc = jnp.zeros((tm, D_MODEL), F32)
    ahead = project(*_FF_CHUNKS[0])
    for i, (c0, w) in enumerate(_FF_CHUNKS):
        cols = slice(c0, c0 + w)
        gpre, up = ahead
        if i + 1 < len(_FF_CHUNKS):
            ahead = project(*_FF_CHUNKS[i + 1])
        gbuf[SUBLANES:SUBLANES + tm, cols] = gpre
        cw = cw_ref[:, cols]
        conv = (cw[0:1, :] * gbuf[SUBLANES - 2:SUBLANES - 2 + tm, cols]
                + cw[1:2, :] * gbuf[SUBLANES - 1:SUBLANES - 1 + tm, cols]
                + cw[2:3, :] * gpre) + cb_ref[:, cols]
        gbuf[0:SUBLANES, cols] = gbuf[tm:tm + SUBLANES, cols]
        act = conv * jax.nn.sigmoid(conv) * up
        acc = acc + _dot(act.astype(BF16), wout_ref[c0:c0 + w, :])
    out_ref[...] = x + acc


def _mix_ffn_call(x2, ya, yb, gm, wa, wb, wo, g2, win, conv_w, conv_b, wout, seq):
    n = x2.shape[0]
    tm = min(ROW_TILE, seq)
    row = lambda w: pl.BlockSpec((tm, w), lambda i: (i, 0))
    full = lambda a: pl.BlockSpec(a.shape, lambda i: (0,) * a.ndim)
    once = lambda a: pl.BlockSpec(a.shape, lambda i: (0,) * a.ndim, pipeline_mode=pl.Buffered(1))
    return pl.pallas_call(
        functools.partial(_mix_ffn_kernel, tiles_per_seq=seq // tm, tm=tm),
        grid=(n // tm,),
        in_specs=[row(D_MODEL), row(CONV_W), row(ATTN_W), row(2 * D_MODEL), once(wa), once(wb), once(wo), full(g2),
                  once(win), full(conv_w), full(conv_b), once(wout)],
        out_specs=row(D_MODEL),
        out_shape=jax.ShapeDtypeStruct((n, D_MODEL), F32),
        scratch_shapes=[pltpu.VMEM((tm + SUBLANES, D_FF), F32)],
        compiler_params=pltpu.CompilerParams(dimension_semantics=("arbitrary",), vmem_limit_bytes=VMEM_LIMIT),
        name="mix_ffn",
    )(x2, ya, yb, gm, wa, wb, wo, g2, win, conv_w, conv_b, wout)


def _block_diag(*blocks):
    rows = []
    for i, blk in enumerate(blocks):
        rows.append(jnp.concatenate([blk if j == i else jnp.zeros_like(blk) for j in range(len(blocks))], axis=-1))
    return jnp.concatenate(rows, axis=-2)


def _static_tables(seq):
    nc, nb = seq // CMP_STRIDE, seq // SEL_BLOCK
    n_cmp = (seq - CMP_BLOCK) // CMP_STRIDE + 1
    ii = np.arange(nc)[None, :] * CMP_STRIDE
    jj = np.arange(nb)[:, None] * SEL_BLOCK
    ovt = ((ii < jj + SEL_BLOCK) & (ii + CMP_BLOCK > jj) & (np.arange(nc)[None, :] < n_cmp)).astype(np.float32)
    ova = np.ones((nc, LANES), np.float32)
    ova[:, :HEAD_DIM] = 0.0
    ova[:, :nb] = ovt.T
    hot = (np.arange(LANES)[None, :] == (np.arange(seq) // SEL_BLOCK)[:, None]).astype(np.float32)
    head = np.arange(SEG_W) // HEAD_DIM
    seg = (head[:, None] == head[None, :]).astype(np.float32)
    return jnp.asarray(ova, BF16), jnp.asarray(hot, BF16), jnp.asarray(seg, BF16)


def kernel(x, positions, norm1_g, w_in, conv_a_w, w_a_out, q_norm_g, k_norm_g, cmp_k_pe, cmp_k_w1, cmp_k_w2, cmp_v_pe, cmp_v_w1, cmp_v_w2, w_b_out, w_o, norm2_g, w_ffn_in, ffn_conv_w, ffn_conv_b, w_ffn_out):
    b, seq, d = x.shape
    n = b * seq
    half = HEAD_DIM // 2
    inv = ROPE_THETA ** (-jnp.arange(half, dtype=F32) / half)
    inv_lanes = jnp.tile(inv, LANES // half)[None, :]
    tm = min(ROW_TILE, seq)
    pos3 = positions.astype(F32).reshape(n // tm, tm // LANES, LANES)
    r_ang = jnp.arange(tm, dtype=F32)[:, None] * inv_lanes
    rcos, rsin = jnp.cos(r_ang), jnp.sin(r_ang)
    stepwise = jnp.all(pos3.reshape(n // tm, tm) == pos3[:, 0, 0:1] + jnp.arange(tm, dtype=F32)[None, :])
    ova, hot, seg = _static_tables(seq)
    gate_cols = N_BRANCH * N_HEADS

    x2 = x.reshape(n, d)
    for l in range(w_in.shape[0]):
        w = w_in[l]
        w_main = w[:, :_GATE_OFF].astype(BF16)
        w_gate = jnp.pad(w[:, _GATE_OFF:_GATE_OFF + gate_cols], ((0, 0), (0, GATE_PAD - gate_cols))).astype(BF16)
        w_gm = w[:, _GATE_OFF + gate_cols:].astype(BF16)
        qg = jnp.tile(q_norm_g[l], N_HEADS)[None, :]
        kg = jnp.tile(k_norm_g[l], (1, N_GROUPS))
        per_b = lambda a: a.reshape(b, seq, a.shape[-1])
        w1k, w1v = (w.reshape(CMP_BLOCK, HEAD_DIM, HEAD_DIM) for w in (cmp_k_w1[l], cmp_v_w1[l]))
        cmp_args = (jnp.concatenate([jnp.tile(cmp_k_pe[l], (1, N_GROUPS)), jnp.tile(cmp_v_pe[l], (1, N_GROUPS))], axis=1),
                    _block_diag(w1k, w1k, w1v, w1v).astype(BF16),
                    _block_diag(cmp_k_w2[l], cmp_k_w2[l], cmp_v_w2[l], cmp_v_w2[l]).astype(BF16), kg[0:1], seg)

        def token_mixers(fast, x_rows, *proj_args):
            ya, q, kvc, ksl, vsl0, vsl1, kwn, vwn0, vwn1, gates, gm = _proj_call(fast, seq, x_rows, *proj_args)
            kcmp, vcmp0, vcmp1 = _compress_call(per_b(kvc), *cmp_args)
            yb = _nsa_call(fast, per_b(q), per_b(gates), kcmp, vcmp0, vcmp1, per_b(ksl), per_b(vsl0), per_b(vsl1),
                           per_b(kwn), per_b(vwn0), per_b(vwn1), ova, hot)
            return ya, yb, gm

        reach = HEAD_DIM * (HEAD_DIM ** -0.5 * LOG2E) * jnp.max(jnp.abs(q_norm_g[l])) * jnp.max(jnp.abs(k_norm_g[l]))
        ya, yb, gm = lax.cond(
            stepwise & (reach * 1.02 <= SCORE_BOUND), functools.partial(token_mixers, True),
            functools.partial(token_mixers, False),
            x2, pos3, norm1_g[l][None, :], w_main, w_gate, w_gm, conv_a_w[l], qg, kg[1:3].reshape(1, 2 * KV_W),
            inv_lanes, rcos, rsin, seg)

        x2 = _mix_ffn_call(x2, ya, yb.reshape(n, ATTN_W), gm, w_a_out[l].astype(BF16), w_b_out[l].astype(BF16),
                           w_o[l].astype(BF16), norm2_g[l][None, :], w_ffn_in[l].astype(BF16), ffn_conv_w[l],
                           ffn_conv_b[l][None, :], w_ffn_out[l].astype(BF16), seq)
    return x2.reshape(b, seq, d)
```
